```python
import math
import jax, jax.numpy as jnp
from jax import lax
import numpy as np

D_MODEL = 1024
BATCH = 4
SEQ = 4096
DEPTH = 2

N_META = 16
BLOCK = 128
PAD = BLOCK - N_META
ROPE_THETA = 10000.0
NORM_EPS = 1e-6
MASK_VALUE = -1e30

MLA_HEADS = 8
MLA_Q_LORA = 256
MLA_KV_LORA = 128
MLA_NOPE = 64
MLA_ROPE = 32
MLA_V = 64
MLA_QK = MLA_NOPE + MLA_ROPE
SB_HEADS = 8
SB_HEAD_DIM = 64
DIFF_HEADS = 4
DIFF_HEAD_DIM = 64
DIFF_V_DIM = 2 * DIFF_HEAD_DIM
N_BRANCHES = 3
MLA_OUT = MLA_HEADS * MLA_V
SB_OUT = SB_HEADS * SB_HEAD_DIM
DIFF_QK = DIFF_HEADS * 2 * DIFF_HEAD_DIM
DIFF_OUT = DIFF_HEADS * DIFF_V_DIM
BRANCH_WIDTH = MLA_OUT + SB_OUT + DIFF_OUT
D_FF = 4 * D_MODEL
IN_SIZES = (MLA_Q_LORA, MLA_KV_LORA, MLA_ROPE,
            SB_OUT, SB_OUT, SB_OUT,
            DIFF_QK, DIFF_QK, DIFF_OUT,
            N_BRANCHES * D_MODEL)
IN_COLS = sum(IN_SIZES)

kernel_name = "hybrid_mla_stickbreak_diffattn_gated"


def _split_cols(t, sizes):
    outs, off = [], 0
    for s in sizes:
        outs.append(t[..., off:off + s])
        off += s
    return outs


def _rmsnorm(x, g):
    x32 = x.astype(jnp.float32)
    y = x32 * lax.rsqrt(jnp.mean(x32 * x32, axis=-1, keepdims=True) + NORM_EPS)
    return (y * g.astype(jnp.float32)).astype(x.dtype)


def _rope(x, pos):
    d = x.shape[-1]
    half = d // 2
    inv_freq = jnp.exp(-math.log(ROPE_THETA) * (2.0 * jnp.arange(half, dtype=jnp.float32) / d))
    ang = pos.astype(jnp.float32)[:, None] * inv_freq[None, :]
    cos = jnp.cos(ang).astype(x.dtype)
    sin = jnp.sin(ang).astype(x.dtype)
    x1, x2 = x[..., :half], x[..., half:]
    return jnp.concatenate([x1 * cos - x2 * sin, x1 * sin + x2 * cos], axis=-1)


def _heads(t, n_heads):
    b, p, _ = t.shape
    return t.reshape(b, p, n_heads, -1).transpose(0, 2, 1, 3)


def _to_blocks(t):
    b, h, p, d = t.shape
    return t.reshape(b, h, p // BLOCK, BLOCK, d).transpose(2, 0, 1, 3, 4)


def _from_blocks(o):
    nb, b, h, blk, d = o.shape
    return o.transpose(1, 0, 3, 2, 4).reshape(b, nb * blk, h * d)


def _scores(qb, k, scale):
    return jnp.einsum('bhqd,bhkd->bhqk', qb, k).astype(jnp.float32) * scale


def _masked_softmax(s, mask):
    return jax.nn.softmax(jnp.where(mask[None, None], s, MASK_VALUE), axis=-1)


def _softmax_attention(q, k, v, scale):
    p_len = q.shape[2]
    kidx = jnp.arange(p_len)
    kvalid = kidx >= PAD

    def block(args):
        qb, qi = args
        mask = (kidx[None, :] <= qi[:, None]) & kvalid[None, :]
        p = _masked_softmax(_scores(qb, k, scale), mask)
        return jnp.einsum('bhqk,bhkd->bhqd', p.astype(v.dtype), v)

    return _from_blocks(lax.map(block, (_to_blocks(q), kidx.reshape(-1, BLOCK))))


def _stick_breaking_attention(q, k, v):
    p_len = q.shape[2]
    scale = q.shape[-1] ** -0.5
    kidx = jnp.arange(p_len)
    kvalid = kidx >= PAD

    def block(args):
        qb, qi = args
        z = _scores(qb, k, scale)
        mask = ((kidx[None, :] < qi[:, None]) & kvalid[None, :])[None, None]
        log_keep = jnp.where(mask, jax.nn.log_sigmoid(-z), 0.0)
        suffix = lax.cumsum(log_keep, axis=3, reverse=True) - log_keep
        a = jnp.where(mask, jnp.exp(jax.nn.log_sigmoid(z) + suffix), 0.0)
        return jnp.einsum('bhqk,bhkd->bhqd', a.astype(v.dtype), v)

    return _from_blocks(lax.map(block, (_to_blocks(q), kidx.reshape(-1, BLOCK))))


def _differential_attention(q1, q2, k1, k2, v, lam):
    p_len = q1.shape[2]
    scale = q1.shape[-1] ** -0.5
    kidx = jnp.arange(p_len)
    kvalid = kidx >= PAD

    def block(args):
        q1b, q2b, qi = args
        mask = (kidx[None, :] <= qi[:, None]) & kvalid[None, :]
        p1 = _masked_softmax(_scores(q1b, k1, scale), mask)
        p2 = _masked_softmax(_scores(q2b, k2, scale), mask)
        w = p1 - lam * p2
        return jnp.einsum('bhqk,bhkd->bhqd', w.astype(v.dtype), v)

    return _from_blocks(lax.map(block, (_to_blocks(q1), _to_blocks(q2), kidx.reshape(-1, BLOCK))))


def setup_inputs(seed: int = 0) -> dict:
    key = jax.random.key(seed)
    ks = jax.random.split(key, 24)
    f32 = jnp.float32

    def nrm(k, shape, scale):
        return jax.random.normal(k, shape, f32) * scale

    def gain(k, shape):
        return 1.0 + 0.02 * jax.random.normal(k, shape, f32)

    return {
        "x": jax.random.normal(ks[0], (BATCH, SEQ, D_MODEL), f32),
        "meta_tokens": nrm(ks[1], (N_META, D_MODEL), 1.0),
        "ln1_g": gain(ks[2], (DEPTH, D_MODEL)),
        "w_in": nrm(ks[3], (DEPTH, D_MODEL, IN_COLS), D_MODEL ** -0.5),
        "mla_cq_norm_g": gain(ks[4], (DEPTH, MLA_Q_LORA)),
        "mla_ckv_norm_g": gain(ks[5], (DEPTH, MLA_KV_LORA)),
        "mla_w_uq": nrm(ks[6], (DEPTH, MLA_Q_LORA, MLA_HEADS * MLA_QK), MLA_Q_LORA ** -0.5),
        "mla_w_ukv": nrm(ks[7], (DEPTH, MLA_KV_LORA, MLA_HEADS * (MLA_NOPE + MLA_V)), MLA_KV_LORA ** -0.5),
        "mla_q_norm_g": gain(ks[8], (DEPTH, MLA_QK)),
        "mla_k_norm_g": gain(ks[9], (DEPTH, MLA_QK)),
        "diff_q_norm_g": gain(ks[10], (DEPTH, DIFF_HEAD_DIM)),
        "diff_k_norm_g": gain(ks[11], (DEPTH, DIFF_HEAD_DIM)),
        "diff_lambda": nrm(ks[12], (DEPTH, 4, DIFF_HEAD_DIM), 0.1),
        "diff_out_norm_g": gain(ks[13], (DEPTH, DIFF_V_DIM)),
        "gate_b": nrm(ks[14], (DEPTH, N_BRANCHES * D_MODEL), 0.02),
        "w_branch": nrm(ks[15], (DEPTH, BRANCH_WIDTH, D_MODEL), MLA_OUT ** -0.5),
        "w_out": nrm(ks[16], (DEPTH, D_MODEL, D_MODEL), D_MODEL ** -0.5),
        "ln2_g": gain(ks[17], (DEPTH, D_MODEL)),
        "w_ff1": nrm(ks[18], (DEPTH, D_MODEL, D_FF), D_MODEL ** -0.5),
        "w_ff2": nrm(ks[19], (DEPTH, D_FF, D_MODEL), D_FF ** -0.5),
    }


def reference(x, meta_tokens, ln1_g, w_in, mla_cq_norm_g, mla_ckv_norm_g, mla_w_uq, mla_w_ukv,
              mla_q_norm_g, mla_k_norm_g, diff_q_norm_g, diff_k_norm_g, diff_lambda,
              diff_out_norm_g, gate_b, w_branch, w_out, ln2_g, w_ff1, w_ff2):
    b, seq, d = x.shape
    pad = jnp.zeros((b, PAD, d), x.dtype)
    meta = jnp.broadcast_to(meta_tokens.astype(x.dtype)[None], (b, N_META, d))
    h_res = jnp.concatenate([pad, meta, x], axis=1)
    p_len = h_res.shape[1]
    pos = jnp.maximum(jnp.arange(p_len) - PAD, 0)

    for layer in range(DEPTH):
        h = _rmsnorm(h_res, ln1_g[layer])
        proj = h @ w_in[layer]
        (c_q, c_kv, k_rope, sb_q, sb_k, sb_v,
         df_q, df_k, df_v, gate_logits) = _split_cols(proj, IN_SIZES)

        q = _heads(_rmsnorm(c_q, mla_cq_norm_g[layer]) @ mla_w_uq[layer], MLA_HEADS)
        kv = _heads(_rmsnorm(c_kv, mla_ckv_norm_g[layer]) @ mla_w_ukv[layer], MLA_HEADS)
        k_nope, v_mla = kv[..., :MLA_NOPE], kv[..., MLA_NOPE:]
        k_r = jnp.broadcast_to(k_rope[:, None], (b, MLA_HEADS, p_len, MLA_ROPE))
        k = jnp.concatenate([k_nope, k_r], axis=-1)
        q = _rmsnorm(q, mla_q_norm_g[layer])
        k = _rmsnorm(k, mla_k_norm_g[layer])
        q = jnp.concatenate([q[..., :MLA_NOPE], _rope(q[..., MLA_NOPE:], pos)], axis=-1)
        k = jnp.concatenate([k[..., :MLA_NOPE], _rope(k[..., MLA_NOPE:], pos)], axis=-1)
        out_a = _softmax_attention(q, k, v_mla, MLA_QK ** -0.5)

        out_b = _stick_breaking_attention(_heads(sb_q, SB_HEADS), _heads(sb_k, SB_HEADS),
                                          _heads(sb_v, SB_HEADS))

        dq = _heads(df_q, DIFF_HEADS)
        dk = _heads(df_k, DIFF_HEADS)
        dv = _heads(df_v, DIFF_HEADS)
        qn, kn = diff_q_norm_g[layer], diff_k_norm_g[layer]
        q1 = _rope(_rmsnorm(dq[..., :DIFF_HEAD_DIM], qn), pos)
        q2 = _rope(_rmsnorm(dq[..., DIFF_HEAD_DIM:], qn), pos)
        k1 = _rope(_rmsnorm(dk[..., :DIFF_HEAD_DIM], kn), pos)
        k2 = _rope(_rmsnorm(dk[..., DIFF_HEAD_DIM:], kn), pos)
        lam_init = 0.8 - 0.6 * math.exp(-0.3 * layer)
        lp = diff_lambda[layer].astype(jnp.float32)
        lam = jnp.exp(jnp.sum(lp[0] * lp[1])) - jnp.exp(jnp.sum(lp[2] * lp[3])) + lam_init
        o_c = _differential_attention(q1, q2, k1, k2, dv, lam)
        o_c = _rmsnorm(o_c.reshape(b, p_len, DIFF_HEADS, DIFF_V_DIM), diff_out_norm_g[layer])
        out_c = (o_c * (1.0 - lam_init)).reshape(b, p_len, DIFF_OUT)

        gates = jax.nn.sigmoid(gate_logits + gate_b[layer]).reshape(b, p_len, N_BRANCHES, d)
        wb_a, wb_b, wb_c = _split_cols(w_branch[layer].T, (MLA_OUT, SB_OUT, DIFF_OUT))
        merged = (gates[:, :, 0] * (out_a @ wb_a.T)
                  + gates[:, :, 1] * (out_b @ wb_b.T)
                  + gates[:, :, 2] * (out_c @ wb_c.T))
        h_res = h_res + merged @ w_out[layer]

        h2 = _rmsnorm(h_res, ln2_g[layer])
        h_res = h_res + jnp.square(jax.nn.relu(h2 @ w_ff1[layer])) @ w_ff2[layer]

    return h_res[:, PAD + N_META:]
```

```python
import functools
import math

import jax
import jax.numpy as jnp
from jax import lax
from jax.experimental import pallas as pl
from jax.experimental.pallas import tpu as pltpu

D_MODEL = 1024
DEPTH = 2
N_META = 16
BLOCK = 128
PAD = BLOCK - N_META
ROPE_THETA = 10000.0
NORM_EPS = 1e-6
MASK_VALUE = -1e30

MLA_HEADS = 8
MLA_Q_LORA = 256
MLA_KV_LORA = 128
MLA_NOPE = 64
MLA_ROPE = 32
MLA_V = 64
MLA_QK = MLA_NOPE + MLA_ROPE
SB_HEADS = 8
SB_HEAD_DIM = 64
DIFF_HEADS = 4
DIFF_HEAD_DIM = 64
DIFF_V_DIM = 2 * DIFF_HEAD_DIM
N_BRANCHES = 3
MLA_OUT = MLA_HEADS * MLA_V
SB_OUT = SB_HEADS * SB_HEAD_DIM
DIFF_QK = DIFF_HEADS * 2 * DIFF_HEAD_DIM
DIFF_OUT = DIFF_HEADS * DIFF_V_DIM
D_FF = 4 * D_MODEL
IN_SIZES = (MLA_Q_LORA, MLA_KV_LORA, MLA_ROPE, SB_OUT, SB_OUT, SB_OUT,
            DIFF_QK, DIFF_QK, DIFF_OUT, N_BRANCHES * D_MODEL)

LANES = 128
HALF_LANES = LANES // 2

LOG2E = math.log2(math.e)
VMEM_LIMIT_BYTES = 56 * 1024 * 1024

_NT = (((1,), (1,)), ((), ()))


def _tiles(p_len):
    assert p_len % (3 * BLOCK) == 0 and p_len % 528 == 0, p_len
    return dict(tm=528, t_soft=3 * BLOCK, t_sb=BLOCK)


def _rms(x, g, n):
    ss = jnp.sum(x * x, axis=-1, keepdims=True) * (1.0 / n)
    return (x * lax.rsqrt(ss + NORM_EPS)) * g


def _rope_lanes(x, cos, sina, sinb, half):
    return x * cos + pltpu.roll(x, LANES - half, 1) * sina + pltpu.roll(x, half, 1) * sinb


def _bdot(a, b):
    return jnp.dot(a, b, preferred_element_type=jnp.float32)


def _pre_kernel(x_ref, g1_ref, wlat_ref, gcq_ref, gckv_ref, wuq_ref, wk_ref, wv_ref,
                gq_ref, gk_ref, cosa_ref, sinaa_ref, sinba_ref,
                wsb_ref, wdf_ref, gdq_ref, gdk_ref, cosc_ref, sinac_ref, sinbc_ref,
                qa_ref, ka_ref, va_ref, qb_ref, kb_ref, vb_ref, qc_ref, kc_ref, vc_ref):
    bf = jnp.bfloat16
    x = x_ref[...]
    h = _rms(x, g1_ref[...], D_MODEL).astype(bf)

    lat = _bdot(h, wlat_ref[...])
    cq = _rms(lat[:, :MLA_Q_LORA], gcq_ref[...], MLA_Q_LORA).astype(bf)
    ckv = _rms(lat[:, MLA_Q_LORA:MLA_Q_LORA + MLA_KV_LORA], gckv_ref[...], MLA_KV_LORA).astype(bf)
    kr_block = lat[:, MLA_Q_LORA + MLA_KV_LORA:]
    qf = _bdot(cq, wuq_ref[...])
    kf = _bdot(ckv, wk_ref[...])
    va_ref[...] = _bdot(ckv, wv_ref[...]).astype(bf)
    cosa, sinaa, sinba = cosa_ref[...], sinaa_ref[...], sinba_ref[...]
    gq, gk = gq_ref[...], gk_ref[...]
    q_scale = (MLA_QK ** -0.5) * LOG2E
    for hh in range(MLA_HEADS):
        sl = slice(hh * LANES, (hh + 1) * LANES)
        qh = _rope_lanes(_rms(qf[:, sl], gq, MLA_QK), cosa, sinaa, sinba, MLA_ROPE // 2)
        qa_ref[:, sl] = (qh * q_scale).astype(bf)
        kh = _rope_lanes(_rms(kf[:, sl] + kr_block, gk, MLA_QK), cosa, sinaa, sinba, MLA_ROPE // 2)
        ka_ref[:, sl] = kh.astype(bf)

    sb = _bdot(h, wsb_ref[...])
    qb_ref[...] = (sb[:, :SB_OUT] * (SB_HEAD_DIM ** -0.5)).astype(bf)
    kb_ref[...] = sb[:, SB_OUT:2 * SB_OUT].astype(bf)
    vb_ref[...] = sb[:, 2 * SB_OUT:].astype(bf)

    df = _bdot(h, wdf_ref[...])
    cosc, sinac, sinbc = cosc_ref[...], sinac_ref[...], sinbc_ref[...]
    lane = lax.broadcasted_iota(jnp.int32, (1, LANES), 1)
    lo = lane < HALF_LANES
    c_scale = (DIFF_HEAD_DIM ** -0.5) * LOG2E

    def half_norm(t, g):
        sq = t * t
        ss_lo = jnp.sum(jnp.where(lo, sq, 0.0), axis=-1, keepdims=True)
        ss_hi = jnp.sum(jnp.where(lo, 0.0, sq), axis=-1, keepdims=True)
        r = jnp.where(lo, lax.rsqrt(ss_lo * (1.0 / DIFF_HEAD_DIM) + NORM_EPS),
                      lax.rsqrt(ss_hi * (1.0 / DIFF_HEAD_DIM) + NORM_EPS))
        return (t * r) * g

    for hh in range(DIFF_HEADS):
        sl = slice(hh * LANES, (hh + 1) * LANES)
        qh = _rope_lanes(half_norm(df[:, sl], gdq_ref[...]), cosc, sinac, sinbc, DIFF_HEAD_DIM // 2)
        qc_ref[:, sl] = (qh * c_scale).astype(bf)
        sk = slice(DIFF_QK + hh * LANES, DIFF_QK + (hh + 1) * LANES)
        kh = _rope_lanes(half_norm(df[:, sk], gdk_ref[...]), cosc, sinac, sinbc, DIFF_HEAD_DIM // 2)
        kc_ref[:, sl] = kh.astype(bf)
    vc_ref[...] = df[:, 2 * DIFF_QK:].astype(bf)


def _tile_mask(i, j, tq, tk, strict):
    qpos = i * tq + lax.broadcasted_iota(jnp.int32, (tq, 1), 0)
    kpos = j * tk + lax.broadcasted_iota(jnp.int32, (1, tk), 1)
    causal = (kpos < qpos) if strict else (kpos <= qpos)
    return causal & (kpos >= PAD)


def _softmax2_kernel(*refs, mode, tq, tk, lam_init=None):
    if mode == 'mla':
        q_ref, k_ref, v_ref, o_ref = refs
    else:
        q_ref, k_ref, v_ref, lam_ref, g_ref, o_ref = refs
    bf = jnp.bfloat16
    i = pl.program_id(2)
    lane = lax.broadcasted_iota(jnp.int32, (1, LANES), 1)
    lo = lane < HALF_LANES

    if mode == 'mla':
        qs = (q_ref[:, :LANES], q_ref[:, LANES:])
    else:
        q = q_ref[...]
        zero = jnp.zeros_like(q)
        qs = (jnp.where(lo, q, zero), jnp.where(lo, zero, q))

    def k_block(j, m):
        rows = pl.ds(pl.multiple_of(j * tk, tk), tk)
        if mode == 'mla':
            return k_ref[rows, m * LANES:(m + 1) * LANES]
        return k_ref[rows, :]

    def step(j, state, masked):
        vblk = v_ref[pl.ds(pl.multiple_of(j * tk, tk), tk), :]
        mask = _tile_mask(i, j, tq, tk, strict=False) if masked else None
        new = []
        for m in range(2):
            m_run, l_run, acc = state[m]
            s = lax.dot_general(qs[m], k_block(j, m), _NT, preferred_element_type=jnp.float32)
            if masked:
                s = jnp.where(mask, s, MASK_VALUE)
            m_new = jnp.maximum(m_run, jnp.max(s, axis=-1, keepdims=True))
            alpha = jnp.exp2(m_run - m_new)
            p = jnp.exp2(s - m_new)
            l_new = alpha * l_run + jnp.sum(p, axis=-1, keepdims=True)
            acc_new = alpha * acc + _bdot(p.astype(bf), vblk)
            new.append((m_new, l_new, acc_new))
        return tuple(new)

    init = tuple((jnp.full((tq, 1), MASK_VALUE, jnp.float32), jnp.zeros((tq, 1), jnp.float32),
                  jnp.zeros((tq, LANES), jnp.float32)) for _ in range(2))
    state = step(0, init, True)
    state = lax.fori_loop(1, i, lambda j, st: step(j, st, False), state)
    state = lax.cond(i > 0, lambda st: step(i, st, True), lambda st: st, state)

    (_, l0, a0), (_, l1, a1) = state
    o0 = a0 / l0
    o1 = a1 / l1
    if mode == 'mla':
        o_ref[...] = jnp.where(lo, o0, o1).astype(bf)
    else:
        lp = lam_ref[...]
        lam =(jnp.exp(jnp.sum(lp[0:1] * lp[1:2], axis=-1, keepdims=True))
               - jnp.exp(jnp.sum(lp[2:3] * lp[3:4], axis=-1, keepdims=True)) + lam_init)
        o = o0 - lam * o1
        o = _rms(o, g_ref[...], DIFF_V_DIM) * (1.0 - lam_init)
        o_ref[...] = o.astype(bf)


def _stick_kernel(q_ref, k_ref, v_ref, tri_ref, o_ref, *, tq, tk):
    bf = jnp.bfloat16
    i = pl.program_id(2)
    lane = lax.broadcasted_iota(jnp.int32, (1, LANES), 1)
    lo = lane < HALF_LANES
    q = q_ref[...]
    zero = jnp.zeros_like(q)
    qs = (jnp.where(lo, q, zero), jnp.where(lo, zero, q))
    tri2 = tri_ref[...]

    def step(j, state, masked):
        rows = pl.ds(pl.multiple_of(j * tk, tk), tk)
        kblk = k_ref[rows, :]
        vblk = v_ref[rows, :]
        mask = _tile_mask(i, j, tq, tk, strict=True) if masked else None
        new = []
        for m in range(2):
            carry, acc = state[m]
            z = lax.dot_general(qs[m], kblk, _NT, preferred_element_type=jnp.float32)
            log_keep = -(jnp.maximum(z, 0.0) + jnp.log(1.0 + jnp.exp(-jnp.abs(z))))
            if masked:
                log_keep = jnp.where(mask, log_keep, 0.0)
            hi = log_keep.astype(bf)
            lw = (log_keep - hi.astype(jnp.float32)).astype(bf)
            suffix = carry + _bdot(jnp.concatenate([hi, lw], axis=1), tri2)
            a = jnp.exp(z + suffix)
            if masked:
                a = jnp.where(mask, a, 0.0)
            acc = acc + _bdot(a.astype(bf), vblk)
            carry = carry + jnp.sum(log_keep, axis=-1, keepdims=True)
            new.append((carry, acc))
        return tuple(new)

    init = tuple((jnp.zeros((tq, 1), jnp.float32), jnp.zeros((tq, LANES), jnp.float32)) for _ in range(2))
    state = step(i, init, True)
    state = lax.fori_loop(1, i, lambda t, st: step(i - t, st, False), state)
    state = lax.cond(i > 0, lambda st: step(0, st, True), lambda st: st, state)
    (_, a0), (_, a1) = state
    o_ref[...] = jnp.where(lo, a0, a1).astype(bf)


def _merge_kernel(x_ref, g1_ref, wg_ref, gb_ref, oa_ref, ob_ref, oc_ref, wb_ref, wo_ref, y_ref):
    bf = jnp.bfloat16
    x = x_ref[...]
    h = _rms(x, g1_ref[...], D_MODEL).astype(bf)
    outs = (oa_ref, ob_ref, oc_ref)
    merged = None
    for br in range(N_BRANCHES):
        cs = slice(br * D_MODEL, (br + 1) * D_MODEL)
        logits = _bdot(h, wg_ref[:, cs]) + gb_ref[:, cs]
        gate = 1.0 / (1.0 + jnp.exp(-logits))
        rs = slice(br * MLA_OUT, (br + 1) * MLA_OUT)
        term = gate * _bdot(outs[br][...], wb_ref[rs, :])
        merged = term if merged is None else merged + term
    y_ref[...] = x + _bdot(merged.astype(bf), wo_ref[...])


def _ffn_kernel(x_ref, g2_ref, w1_ref, w2_ref, y_ref, *, n_chunks):
    bf = jnp.bfloat16
    x = x_ref[...]
    h = _rms(x, g2_ref[...], D_MODEL).astype(bf)
    ck = D_FF // n_chunks
    acc = x
    for c in range(n_chunks):
        cs = slice(c * ck, (c + 1) * ck)
        f = jnp.maximum(_bdot(h, w1_ref[:, cs]), 0.0)
        acc = acc + _bdot((f * f).astype(bf), w2_ref[cs, :])
    y_ref[...] = acc


def _rope_tables(p_len, d, blocks):
    half = d // 2
    pos = jnp.maximum(jnp.arange(p_len) - PAD, 0)
    inv_freq = jnp.exp(-math.log(ROPE_THETA) * (2.0 * jnp.arange(half, dtype=jnp.float32) / d))
    ang = pos.astype(jnp.float32)[:, None] * inv_freq[None, :]
    cos, sin = jnp.cos(ang), jnp.sin(ang)
    cos_t = jnp.ones((p_len, LANES), jnp.float32)
    sina_t = jnp.zeros((p_len, LANES), jnp.float32)
    sinb_t = jnp.zeros((p_len, LANES), jnp.float32)
    for off in blocks:
        cos_t = cos_t.at[:, off:off + half].set(cos).at[:, off + half:off + d].set(cos)
        sina_t = sina_t.at[:, off:off + half].set(-sin)
        sinb_t = sinb_t.at[:, off + half:off + d].set(sin)
    return cos_t, sina_t, sinb_t


def _row(v):
    return v.reshape(1, -1).astype(jnp.float32)


def _full(shape):
    return pl.BlockSpec(shape, lambda *_: (0,) * len(shape))


def _layer_weights(layer, w_in, mla_w_uq, mla_w_ukv, w_branch, w_out, w_ff1, w_ff2):
    bf = jnp.bfloat16
    w = w_in[layer]
    offs = [0]
    for s in IN_SIZES:
        offs.append(offs[-1] + s)
    d = w.shape[0]
    w_lat = jnp.concatenate([
        w[:, offs[0]:offs[2]],
        jnp.zeros((d, MLA_NOPE), w.dtype), w[:, offs[2]:offs[3]],
        jnp.zeros((d, LANES - MLA_QK), w.dtype)], axis=1).astype(bf)
    w_sb = w[:, offs[3]:offs[6]].astype(bf)
    w_df = w[:, offs[6]:offs[9]].astype(bf)
    w_gate = w[:, offs[9]:offs[10]].astype(bf)
    wuq = mla_w_uq[layer].reshape(MLA_Q_LORA, MLA_HEADS, MLA_QK)
    wuq = jnp.pad(wuq, ((0, 0), (0, 0), (0, LANES - MLA_QK))).reshape(MLA_Q_LORA, MLA_HEADS * LANES).astype(bf)
    wukv = mla_w_ukv[layer].reshape(MLA_KV_LORA, MLA_HEADS, MLA_NOPE + MLA_V)
    wk = jnp.pad(wukv[:, :, :MLA_NOPE], ((0, 0), (0, 0), (0, LANES - MLA_NOPE)))
    wk = wk.reshape(MLA_KV_LORA, MLA_HEADS * LANES).astype(bf)
    wv = wukv[:, :, MLA_NOPE:].reshape(MLA_KV_LORA, MLA_OUT).astype(bf)
    return dict(w_lat=w_lat, w_sb=w_sb, w_df=w_df, w_gate=w_gate, wuq=wuq, wk=wk, wv=wv,
                wb=w_branch[layer].astype(bf), wo=w_out[layer].astype(bf),
                w1=w_ff1[layer].astype(bf), w2=w_ff2[layer].astype(bf))


def kernel(x, meta_tokens, ln1_g, w_in, mla_cq_norm_g, mla_ckv_norm_g, mla_w_uq, mla_w_ukv,
           mla_q_norm_g, mla_k_norm_g, diff_q_norm_g, diff_k_norm_g, diff_lambda,
           diff_out_norm_g, gate_b, w_branch, w_out, ln2_g, w_ff1, w_ff2):
    b, seq, d = x.shape
    assert d == D_MODEL
    bf = jnp.bfloat16
    pad = jnp.zeros((b, PAD, d), x.dtype)
    meta = jnp.broadcast_to(meta_tokens.astype(x.dtype)[None], (b, N_META, d))
    h_res = jnp.concatenate([pad, meta, x], axis=1)
    p_len = h_res.shape[1]
    t = _tiles(p_len)
    tm, tqs, tqb = t['tm'], t['t_soft'], t['t_sb']
    n_tm = p_len // tm

    tab_a = _rope_tables(p_len, MLA_ROPE, [MLA_NOPE])
    tab_c = _rope_tables(p_len, DIFF_HEAD_DIM, [0, HALF_LANES])
    tri = (jnp.arange(tqb)[:, None] >= jnp.arange(tqb)[None, :]).astype(bf)
    tri2 = jnp.concatenate([tri, tri], axis=0)

    cparams2 = pltpu.CompilerParams(dimension_semantics=("parallel", "parallel"),
                                    vmem_limit_bytes=VMEM_LIMIT_BYTES)
    cparams3 = pltpu.CompilerParams(dimension_semantics=("parallel", "parallel", "arbitrary"),
                                    vmem_limit_bytes=VMEM_LIMIT_BYTES)

    def tok(width):
        return pl.BlockSpec((None, tm, width), lambda bi, ti: (bi, ti, 0))

    def tab():
        return pl.BlockSpec((tm, LANES), lambda bi, ti: (ti, 0))

    def act(width):
        return jax.ShapeDtypeStruct((b, p_len, width), bf)

    for layer in range(DEPTH):
        wts = _layer_weights(layer, w_in, mla_w_uq, mla_w_ukv, w_branch, w_out, w_ff1, w_ff2)
        g1 = _row(ln1_g[layer])
        gq = _row(jnp.pad(mla_q_norm_g[layer], (0, LANES - MLA_QK)))
        gk = _row(jnp.pad(mla_k_norm_g[layer], (0, LANES - MLA_QK)))
        gdq = _row(jnp.tile(diff_q_norm_g[layer], 2))
        gdk = _row(jnp.tile(diff_k_norm_g[layer], 2))

        pre_in = [h_res, g1, wts['w_lat'], _row(mla_cq_norm_g[layer]), _row(mla_ckv_norm_g[layer]),
                  wts['wuq'], wts['wk'], wts['wv'], gq, gk, *tab_a,
                  wts['w_sb'], wts['w_df'], gdq, gdk, *tab_c]
        pre_specs = [tok(d)] + [_full(a.shape) for a in pre_in[1:10]] + [tab()] * 3 \
            + [_full(a.shape) for a in pre_in[13:17]] + [tab()] * 3
        widths = (MLA_HEADS * LANES, MLA_HEADS * LANES, MLA_OUT, SB_OUT, SB_OUT, SB_OUT,
                  DIFF_QK, DIFF_QK, DIFF_OUT)
        qa, ka, va, qb, kb, vb, qc, kc, vc = pl.pallas_call(
            _pre_kernel, grid=(b, n_tm), in_specs=pre_specs,
            out_specs=[tok(wd) for wd in widths], out_shape=[act(wd) for wd in widths],
            compiler_params=cparams2, name="pre_tokens")(*pre_in)

        n_q = p_len // tqs
        out_a = pl.pallas_call(
            functools.partial(_softmax2_kernel, mode='mla', tq=tqs, tk=tqs),
            grid=(b, MLA_HEADS // 2, n_q),
            in_specs=[pl.BlockSpec((None, tqs, 2 * LANES), lambda bi, hp, qi: (bi, qi, hp)),
                      pl.BlockSpec((None, p_len, 2 * LANES), lambda bi, hp, qi: (bi, 0, hp)),
                      pl.BlockSpec((None, p_len, LANES), lambda bi, hp, qi: (bi, 0, hp))],
            out_specs=pl.BlockSpec((None, tqs, LANES), lambda bi, hp, qi: (bi, qi, hp)),
            out_shape=act(MLA_OUT), compiler_params=cparams3, name="attn_mla")(qa, ka, va)

        n_qb = p_len // tqb
        out_b = pl.pallas_call(
            functools.partial(_stick_kernel, tq=tqb, tk=tqb),
            grid=(b, SB_HEADS // 2, n_qb),
            in_specs=[pl.BlockSpec((None, tqb, LANES), lambda bi, hp, qi: (bi, qi, hp)),
                      pl.BlockSpec((None, p_len, LANES), lambda bi, hp, qi: (bi, 0, hp)),
                      pl.BlockSpec((None, p_len, LANES), lambda bi, hp, qi: (bi, 0, hp)),
                      _full(tri2.shape)],
            out_specs=pl.BlockSpec((None, tqb, LANES), lambda bi, hp, qi: (bi, qi, hp)),
            out_shape=act(SB_OUT), compiler_params=cparams3, name="attn_stick")(qb, kb, vb, tri2)

        lam_init = 0.8 - 0.6 * math.exp(-0.3 * layer)
        g_c = _row(diff_out_norm_g[layer])
        out_c = pl.pallas_call(
            functools.partial(_softmax2_kernel, mode='diff', tq=tqs, tk=tqs, lam_init=lam_init),
            grid=(b, DIFF_HEADS, n_q),
            in_specs=[pl.BlockSpec((None, tqs, LANES), lambda bi, hp, qi: (bi, qi, hp)),
                      pl.BlockSpec((None, p_len, LANES), lambda bi, hp, qi: (bi, 0, hp)),
                      pl.BlockSpec((None, p_len, LANES), lambda bi, hp, qi: (bi, 0, hp)),
                      _full((4, DIFF_HEAD_DIM)), _full((1, LANES))],
            out_specs=pl.BlockSpec((None, tqs, LANES), lambda bi, hp, qi: (bi, qi, hp)),
            out_shape=act(DIFF_OUT), compiler_params=cparams3, name="attn_diff")(
                qc, kc, vc, diff_lambda[layer].astype(jnp.float32), g_c)

        mg_in = [h_res, g1, wts['w_gate'], _row(gate_b[layer]), out_a, out_b, out_c, wts['wb'], wts['wo']]
        mg_specs = [tok(d), _full(g1.shape), _full(wts['w_gate'].shape), _full((1, N_BRANCHES * D_MODEL)),
                    tok(MLA_OUT), tok(SB_OUT), tok(DIFF_OUT), _full(wts['wb'].shape), _full(wts['wo'].shape)]
        h_res = pl.pallas_call(
            _merge_kernel, grid=(b, n_tm), in_specs=mg_specs, out_specs=tok(d),
            out_shape=jax.ShapeDtypeStruct((b, p_len, d), jnp.float32),
            compiler_params=cparams2, name="merge_out")(*mg_in)

        g2 = _row(ln2_g[layer])
        h_res = pl.pallas_call(
            functools.partial(_ffn_kernel, n_chunks=4), grid=(b, n_tm),
            in_specs=[tok(d), _full(g2.shape), _full(wts['w1'].shape), _full(wts['w2'].shape)],
            out_specs=tok(d), out_shape=jax.ShapeDtypeStruct((b, p_len, d), jnp.float32),
            compiler_params=cparams2, name="ffn")(h_res, g2, wts['w1'], wts['w2'])

    return h_res[:, PAD + N_META:]
```

```python
import functools
import math

import jax
import jax.numpy as jnp
from jax import lax
from jax.experimental import pallas as pl
from jax.experimental.pallas import tpu as pltpu

D_MODEL = 1024
DEPTH = 2
N_META = 16
BLOCK = 128
PAD = BLOCK - N_META
ROPE_THETA = 10000.0
NORM_EPS = 1e-6
MASK_VALUE = -1e30

MLA_HEADS = 8
MLA_Q_LORA = 256
MLA_KV_LORA = 128
MLA_NOPE = 64
MLA_ROPE = 32
MLA_V = 64
MLA_QK = MLA_NOPE + MLA_ROPE
SB_HEADS = 8
SB_HEAD_DIM = 64
DIFF_HEADS = 4
DIFF_HEAD_DIM = 64
DIFF_V_DIM = 2 * DIFF_HEAD_DIM
N_BRANCHES = 3
MLA_OUT = MLA_HEADS * MLA_V
SB_OUT = SB_HEADS * SB_HEAD_DIM
DIFF_QK = DIFF_HEADS * 2 * DIFF_HEAD_DIM
DIFF_OUT = DIFF_HEADS * DIFF_V_DIM
D_FF = 4 * D_MODEL
IN_SIZES = (MLA_Q_LORA, MLA_KV_LORA, MLA_ROPE, SB_OUT, SB_OUT, SB_OUT,
            DIFF_QK, DIFF_QK, DIFF_OUT, N_BRANCHES * D_MODEL)

LANES = 128
HALF_LANES = LANES // 2

LOG2E = math.log2(math.e)
VMEM_LIMIT_BYTES = 56 * 1024 * 1024
STICK_SKIP = 110.0

_NT = (((1,), (1,)), ((), ()))


def _tiles(p_len):
    assert p_len % (3 * BLOCK) == 0 and p_len % 528 == 0, p_len
    return dict(tm=528, t_soft=3 * BLOCK, t_sb=BLOCK)


def _rms(x, g, n):
    ss = jnp.sum(x * x, axis=-1, keepdims=True) * (1.0 / n)
    return (x * lax.rsqrt(ss + NORM_EPS)) * g


def _rope_lanes(x, cos, sina, sinb, half):
    return x * cos + pltpu.roll(x, LANES - half, 1) * sina + pltpu.roll(x, half, 1) * sinb


def _bdot(a, b):
    return jnp.dot(a, b, preferred_element_type=jnp.float32)


def _pre_kernel(x_ref, g1_ref, wlat_ref, gcq_ref, gckv_ref, wuq_ref, wk_ref, wv_ref,
                gq_ref, gk_ref, cosa_ref, sinaa_ref, sinba_ref,
                wsb_ref, wdf_ref, gdq_ref, gdk_ref, cosc_ref, sinac_ref, sinbc_ref,
                qa_ref, ka_ref, va_ref, qb_ref, kb_ref, vb_ref, qc_ref, kc_ref, vc_ref):
    bf = jnp.bfloat16
    x = x_ref[...]
    h = _rms(x, g1_ref[...], D_MODEL).astype(bf)

    lat = _bdot(h, wlat_ref[...])
    cq = _rms(lat[:, :MLA_Q_LORA], gcq_ref[...], MLA_Q_LORA).astype(bf)
    ckv = _rms(lat[:, MLA_Q_LORA:MLA_Q_LORA + MLA_KV_LORA], gckv_ref[...], MLA_KV_LORA).astype(bf)
    kr_block = lat[:, MLA_Q_LORA + MLA_KV_LORA:]
    qf = _bdot(cq, wuq_ref[...])
    kf = _bdot(ckv, wk_ref[...])
    va_ref[...] = _bdot(ckv, wv_ref[...]).astype(bf)
    cosa, sinaa, sinba = cosa_ref[...], sinaa_ref[...], sinba_ref[...]
    gq, gk = gq_ref[...], gk_ref[...]
    q_scale = (MLA_QK ** -0.5) * LOG2E
    for hh in range(MLA_HEADS):
        sl = slice(hh * LANES, (hh + 1) * LANES)
        qh = _rope_lanes(_rms(qf[:, sl], gq, MLA_QK), cosa, sinaa, sinba, MLA_ROPE // 2)
        qa_ref[:, sl] = (qh * q_scale).astype(bf)
        kh = _rope_lanes(_rms(kf[:, sl] + kr_block, gk, MLA_QK), cosa, sinaa, sinba, MLA_ROPE // 2)
        ka_ref[:, sl] = kh.astype(bf)

    sb = _bdot(h, wsb_ref[...])
    qb_ref[...] = (sb[:, :SB_OUT] * (SB_HEAD_DIM ** -0.5)).astype(bf)
    kb_ref[...] = sb[:, SB_OUT:2 * SB_OUT].astype(bf)
    vb_ref[...] = sb[:, 2 * SB_OUT:].astype(bf)

    df = _bdot(h, wdf_ref[...])
    cosc, sinac, sinbc = cosc_ref[...], sinac_ref[...], sinbc_ref[...]
    lane = lax.broadcasted_iota(jnp.int32, (1, LANES), 1)
    lo = lane < HALF_LANES
    c_scale = (DIFF_HEAD_DIM ** -0.5) * LOG2E

    def half_norm(t, g):
        sq = t * t
        ss_lo = jnp.sum(jnp.where(lo, sq, 0.0), axis=-1, keepdims=True)
        ss_hi = jnp.sum(jnp.where(lo, 0.0, sq), axis=-1, keepdims=True)
        r = jnp.where(lo, lax.rsqrt(ss_lo * (1.0 / DIFF_HEAD_DIM) + NORM_EPS),
                      lax.rsqrt(ss_hi * (1.0 / DIFF_HEAD_DIM) + NORM_EPS))
        return (t * r) * g

    for hh in range(DIFF_HEADS):
        sl = slice(hh * LANES, (hh + 1) * LANES)
        qh = _rope_lanes(half_norm(df[:, sl], gdq_ref[...]), cosc, sinac, sinbc, DIFF_HEAD_DIM // 2)
        qc_ref[:, sl] = (qh * c_scale).astype(bf)
        sk = slice(DIFF_QK + hh * LANES, DIFF_QK + (hh + 1) * LANES)
        kh = _rope_lanes(half_norm(df[:, sk], gdk_ref[...]), cosc, sinac, sinbc, DIFF_HEAD_DIM // 2)
        kc_ref[:, sl] = kh.astype(bf)
    vc_ref[...] = df[:, 2 * DIFF_QK:].astype(bf)


def _tile_mask(i, j, tq, tk, strict):
    qpos = i * tq + lax.broadcasted_iota(jnp.int32, (tq, 1), 0)
    kpos = j * tk + lax.broadcasted_iota(jnp.int32, (1, tk), 1)
    causal = (kpos < qpos) if strict else (kpos <= qpos)
    return causal & (kpos >= PAD)


def _softmax2_kernel(*refs, mode, tq, tk, lam_init=None):
    if mode == 'mla':
        q_ref, k_ref, v_ref, o_ref = refs
    else:
        q_ref, k_ref, v_ref, lam_ref, g_ref, o_ref = refs
    bf = jnp.bfloat16
    i = pl.program_id(2)
    lane = lax.broadcasted_iota(jnp.int32, (1, LANES), 1)
    lo = lane < HALF_LANES

    if mode == 'mla':
        qs = (q_ref[:, :LANES], q_ref[:, LANES:])
    else:
        q = q_ref[...]
        zero = jnp.zeros_like(q)
        qs = (jnp.where(lo, q, zero), jnp.where(lo, zero, q))

    def k_block(j, m):
        rows = pl.ds(pl.multiple_of(j * tk, tk), tk)
        if mode == 'mla':
            return k_ref[rows, m * LANES:(m + 1) * LANES]
        return k_ref[rows, :]

    def step(j, state, masked):
        vblk = v_ref[pl.ds(pl.multiple_of(j * tk, tk), tk), :]
        mask = _tile_mask(i, j, tq, tk, strict=False) if masked else None
        new = []
        for m in range(2):
            m_run, l_run, acc = state[m]
            s = lax.dot_general(qs[m], k_block(j, m), _NT, preferred_element_type=jnp.float32)
            if masked:
                s = jnp.where(mask, s, MASK_VALUE)
            m_new = jnp.maximum(m_run, jnp.max(s, axis=-1, keepdims=True))
            alpha = jnp.exp2(m_run - m_new)
            p = jnp.exp2(s - m_new)
            l_new = alpha * l_run + jnp.sum(p, axis=-1, keepdims=True)
            acc_new = alpha * acc + _bdot(p.astype(bf), vblk)
            new.append((m_new, l_new, acc_new))
        return tuple(new)

    init = tuple((jnp.full((tq, 1), MASK_VALUE, jnp.float32), jnp.zeros((tq, 1), jnp.float32),
                  jnp.zeros((tq, LANES), jnp.float32)) for _ in range(2))
    state = step(0, init, True)
    state = lax.fori_loop(1, i, lambda j, st: step(j, st, False), state)
    state = lax.cond(i > 0, lambda st: step(i, st, True), lambda st: st, state)

    (_, l0, a0), (_, l1, a1) = state
    o0 = a0 / l0
    o1 = a1 / l1
    if mode == 'mla':
        o_ref[...] = jnp.where(lo, o0, o1).astype(bf)
    else:
        lp = lam_ref[...]
        lam =(jnp.exp(jnp.sum(lp[0:1] * lp[1:2], axis=-1, keepdims=True))
               - jnp.exp(jnp.sum(lp[2:3] * lp[3:4], axis=-1, keepdims=True)) + lam_init)
        o = o0 - lam * o1
        o = _rms(o, g_ref[...], DIFF_V_DIM) * (1.0 - lam_init)
        o_ref[...] = o.astype(bf)


def _stick_kernel(q_ref, k_ref, v_ref, tri_ref, o_ref, *, tq, tk):
    bf = jnp.bfloat16
    i = pl.program_id(1)
    n_blocks = q_ref.shape[-1] // LANES
    lane = lax.broadcasted_iota(jnp.int32, (1, LANES), 1)
    lo = lane < HALF_LANES
    qs = []
    for blk in range(n_blocks):
        q = q_ref[:, blk * LANES:(blk + 1) * LANES]
        zero = jnp.zeros_like(q)
        qs += [jnp.where(lo, q, zero), jnp.where(lo, zero, q)]
    tri2 = tri_ref[...]

    def cond(st):
        j, worst, _ = st
        return (j >= 0) & (worst > -STICK_SKIP)

    def body(st):
        j, _, state = st
        rows = pl.ds(pl.multiple_of(j * tk, tk), tk)
        mask = _tile_mask(i, j, tq, tk, strict=True)
        new = []
        worst = None
        for m in range(2 * n_blocks):
            cols = slice((m // 2) * LANES, (m // 2 + 1) * LANES)
            carry, acc = state[m]
            z = lax.dot_general(qs[m], k_ref[rows, cols], _NT, preferred_element_type=jnp.float32)
            log_keep = -(jnp.maximum(z, 0.0) + jnp.log(1.0 + jnp.exp(-jnp.abs(z))))
            log_keep = jnp.where(mask, log_keep, 0.0)
            hi = log_keep.astype(bf)
            lw = (log_keep - hi.astype(jnp.float32)).astype(bf)
            suffix = carry + _bdot(jnp.concatenate([hi, lw], axis=1), tri2)
            a = jnp.where(mask, jnp.exp(z + suffix), 0.0)
            acc = acc + _bdot(a.astype(bf), v_ref[rows, cols])
            carry = carry + jnp.sum(log_keep, axis=-1, keepdims=True)
            new.append((carry, acc))
            worst = carry if worst is None else jnp.maximum(worst, carry)
        return j - 1, jnp.max(worst), tuple(new)

    init = tuple((jnp.zeros((tq, 1), jnp.float32), jnp.zeros((tq, LANES), jnp.float32))
                 for _ in range(2 * n_blocks))
    _, _, state = lax.while_loop(cond, body, (i, jnp.float32(0.0), init))
    for blk in range(n_blocks):
        o_ref[:, blk * LANES:(blk + 1) * LANES] = jnp.where(lo, state[2 * blk][1], state[2 * blk + 1][1]).astype(bf)


def _merge_kernel(x_ref, g1_ref, wg_ref, gb_ref, oa_ref, ob_ref, oc_ref, wb_ref, wo_ref, y_ref):
    bf = jnp.bfloat16
    x = x_ref[...]
    h = _rms(x, g1_ref[...], D_MODEL).astype(bf)
    outs = (oa_ref, ob_ref, oc_ref)
    merged = None
    for br in range(N_BRANCHES):
        cs = slice(br * D_MODEL, (br + 1) * D_MODEL)
        logits = _bdot(h, wg_ref[:, cs]) + gb_ref[:, cs]
        gate = 1.0 / (1.0 + jnp.exp(-logits))
        rs = slice(br * MLA_OUT, (br + 1) * MLA_OUT)
        term = gate * _bdot(outs[br][...], wb_ref[rs, :])
        merged = term if merged is None else merged + term
    y_ref[...] = x + _bdot(merged.astype(bf), wo_ref[...])


def _ffn_kernel(x_ref, g2_ref, w1_ref, w2_ref, y_ref, *, n_chunks):
    bf = jnp.bfloat16
    x = x_ref[...]
    h = _rms(x, g2_ref[...], D_MODEL).astype(bf)
    ck = D_FF // n_chunks
    acc = x
    for c in range(n_chunks):
        cs = slice(c * ck, (c + 1) * ck)
        f = jnp.maximum(_bdot(h, w1_ref[:, cs]), 0.0)
        acc = acc + _bdot((f * f).astype(bf), w2_ref[cs, :])
    y_ref[...] = acc


def _rope_tables(p_len, d, blocks):
    half = d // 2
    pos = jnp.maximum(jnp.arange(p_len) - PAD, 0)
    inv_freq = jnp.exp(-math.log(ROPE_THETA) * (2.0 * jnp.arange(half, dtype=jnp.float32) / d))
    ang = pos.astype(jnp.float32)[:, None] * inv_freq[None, :]
    cos, sin = jnp.cos(ang), jnp.sin(ang)
    cos_t = jnp.ones((p_len, LANES), jnp.float32)
    sina_t = jnp.zeros((p_len, LANES), jnp.float32)
    sinb_t = jnp.zeros((p_len, LANES), jnp.float32)
    for off in blocks:
        cos_t = cos_t.at[:, off:off + half].set(cos).at[:, off + half:off + d].set(cos)
        sina_t = sina_t.at[:, off:off + half].set(-sin)
        sinb_t = sinb_t.at[:, off + half:off + d].set(sin)
    return cos_t, sina_t, sinb_t


def _row(v):
    return v.reshape(1, -1).astype(jnp.float32)


def _full(shape):
    return pl.BlockSpec(shape, lambda *_: (0,) * len(shape))


def _layer_weights(layer, w_in, mla_w_uq, mla_w_ukv, w_branch, w_out, w_ff1, w_ff2):
    bf = jnp.bfloat16
    w = w_in[layer]
    offs = [0]
    for s in IN_SIZES:
        offs.append(offs[-1] + s)
    d = w.shape[0]
    w_lat = jnp.concatenate([
        w[:, offs[0]:offs[2]],
        jnp.zeros((d, MLA_NOPE), w.dtype), w[:, offs[2]:offs[3]],
        jnp.zeros((d, LANES - MLA_QK), w.dtype)], axis=1).astype(bf)
    w_sb = w[:, offs[3]:offs[6]].astype(bf)
    w_df = w[:, offs[6]:offs[9]].astype(bf)
    w_gate = w[:, offs[9]:offs[10]].astype(bf)
    wuq = mla_w_uq[layer].reshape(MLA_Q_LORA, MLA_HEADS, MLA_QK)
    wuq = jnp.pad(wuq, ((0, 0), (0, 0), (0, LANES - MLA_QK))).reshape(MLA_Q_LORA, MLA_HEADS * LANES).astype(bf)
    wukv = mla_w_ukv[layer].reshape(MLA_KV_LORA, MLA_HEADS, MLA_NOPE + MLA_V)
    wk = jnp.pad(wukv[:, :, :MLA_NOPE], ((0, 0), (0, 0), (0, LANES - MLA_NOPE)))
    wk = wk.reshape(MLA_KV_LORA, MLA_HEADS * LANES).astype(bf)
    wv = wukv[:, :, MLA_NOPE:].reshape(MLA_KV_LORA, MLA_OUT).astype(bf)
    return dict(w_lat=w_lat, w_sb=w_sb, w_df=w_df, w_gate=w_gate, wuq=wuq, wk=wk, wv=wv,
                wb=w_branch[layer].astype(bf), wo=w_out[layer].astype(bf),
                w1=w_ff1[layer].astype(bf), w2=w_ff2[layer].astype(bf))


def kernel(x, meta_tokens, ln1_g, w_in, mla_cq_norm_g, mla_ckv_norm_g, mla_w_uq, mla_w_ukv,
           mla_q_norm_g, mla_k_norm_g, diff_q_norm_g, diff_k_norm_g, diff_lambda,
           diff_out_norm_g, gate_b, w_branch, w_out, ln2_g, w_ff1, w_ff2):
    b, seq, d = x.shape
    assert d == D_MODEL
    bf = jnp.bfloat16
    pad = jnp.zeros((b, PAD, d), x.dtype)
    meta = jnp.broadcast_to(meta_tokens.astype(x.dtype)[None], (b, N_META, d))
    h_res = jnp.concatenate([pad, meta, x], axis=1)
    p_len = h_res.shape[1]
    t = _tiles(p_len)
    tm, tqs, tqb = t['tm'], t['t_soft'], t['t_sb']
    n_tm = p_len // tm

    tab_a = _rope_tables(p_len, MLA_ROPE, [MLA_NOPE])
    tab_c = _rope_tables(p_len, DIFF_HEAD_DIM, [0, HALF_LANES])
    tri = (jnp.arange(tqb)[:, None] >= jnp.arange(tqb)[None, :]).astype(bf)
    tri2 = jnp.concatenate([tri, tri], axis=0)

    cparams2 = pltpu.CompilerParams(dimension_semantics=("parallel", "parallel"),
                                    vmem_limit_bytes=VMEM_LIMIT_BYTES)
    cparams2s = pltpu.CompilerParams(dimension_semantics=("parallel", "arbitrary"),
                                     vmem_limit_bytes=VMEM_LIMIT_BYTES)
    cparams3 = pltpu.CompilerParams(dimension_semantics=("parallel", "parallel", "arbitrary"),
                                    vmem_limit_bytes=VMEM_LIMIT_BYTES)

    def tok(width):
        return pl.BlockSpec((None, tm, width), lambda bi, ti: (bi, ti, 0))

    def tab():
        return pl.BlockSpec((tm, LANES), lambda bi, ti: (ti, 0))

    def act(width):
        return jax.ShapeDtypeStruct((b, p_len, width), bf)

    for layer in range(DEPTH):
        wts = _layer_weights(layer, w_in, mla_w_uq, mla_w_ukv, w_branch, w_out, w_ff1, w_ff2)
        g1 = _row(ln1_g[layer])
        gq = _row(jnp.pad(mla_q_norm_g[layer], (0, LANES - MLA_QK)))
        gk = _row(jnp.pad(mla_k_norm_g[layer], (0, LANES - MLA_QK)))
        gdq = _row(jnp.tile(diff_q_norm_g[layer], 2))
        gdk = _row(jnp.tile(diff_k_norm_g[layer], 2))

        pre_in = [h_res, g1, wts['w_lat'], _row(mla_cq_norm_g[layer]), _row(mla_ckv_norm_g[layer]),
                  wts['wuq'], wts['wk'], wts['wv'], gq, gk, *tab_a,
                  wts['w_sb'], wts['w_df'], gdq, gdk, *tab_c]
        pre_specs = [tok(d)] + [_full(a.shape) for a in pre_in[1:10]] + [tab()] * 3 \
            + [_full(a.shape) for a in pre_in[13:17]] + [tab()] * 3
        widths = (MLA_HEADS * LANES, MLA_HEADS * LANES, MLA_OUT, SB_OUT, SB_OUT, SB_OUT,
                  DIFF_QK, DIFF_QK, DIFF_OUT)
        qa, ka, va, qb, kb, vb, qc, kc, vc = pl.pallas_call(
            _pre_kernel, grid=(b, n_tm), in_specs=pre_specs,
            out_specs=[tok(wd) for wd in widths], out_shape=[act(wd) for wd in widths],
            compiler_params=cparams2, name="pre_tokens")(*pre_in)

        n_q = p_len // tqs
        out_a = pl.pallas_call(
            functools.partial(_softmax2_kernel, mode='mla', tq=tqs, tk=tqs),
            grid=(b, MLA_HEADS // 2, n_q),
            in_specs=[pl.BlockSpec((None, tqs, 2 * LANES), lambda bi, hp, qi: (bi, qi, hp)),
                      pl.BlockSpec((None, p_len, 2 * LANES), lambda bi, hp, qi: (bi, 0, hp)),
                      pl.BlockSpec((None, p_len, LANES), lambda bi, hp, qi: (bi, 0, hp))],
            out_specs=pl.BlockSpec((None, tqs, LANES), lambda bi, hp, qi: (bi, qi, hp)),
            out_shape=act(MLA_OUT), compiler_params=cparams3, name="attn_mla")(qa, ka, va)

        n_qb = p_len // tqb
        out_b = pl.pallas_call(
            functools.partial(_stick_kernel, tq=tqb, tk=tqb),
            grid=(b, n_qb),
            in_specs=[pl.BlockSpec((None, tqb, SB_OUT), lambda bi, qi: (bi, qi, 0)),
                      pl.BlockSpec((None, p_len, SB_OUT), lambda bi, qi: (bi, 0, 0)),
                      pl.BlockSpec((None, p_len, SB_OUT), lambda bi, qi: (bi, 0, 0)),
                      _full(tri2.shape)],
            out_specs=pl.BlockSpec((None, tqb, SB_OUT), lambda bi, qi: (bi, qi, 0)),
            out_shape=act(SB_OUT), compiler_params=cparams2s, name="attn_stick")(qb, kb, vb, tri2)

        lam_init = 0.8 - 0.6 * math.exp(-0.3 * layer)
        g_c = _row(diff_out_norm_g[layer])
        out_c = pl.pallas_call(
            functools.partial(_softmax2_kernel, mode='diff', tq=tqs, tk=tqs, lam_init=lam_init),
            grid=(b, DIFF_HEADS, n_q),
            in_specs=[pl.BlockSpec((None, tqs, LANES), lambda bi, hp, qi: (bi, qi, hp)),
                      pl.BlockSpec((None, p_len, LANES), lambda bi, hp, qi: (bi, 0, hp)),
                      pl.BlockSpec((None, p_len, LANES), lambda bi, hp, qi: (bi, 0, hp)),
                      _full((4, DIFF_HEAD_DIM)), _full((1, LANES))],
            out_specs=pl.BlockSpec((None, tqs, LANES), lambda bi, hp, qi: (bi, qi, hp)),
            out_shape=act(DIFF_OUT), compiler_params=cparams3, name="attn_diff")(
                qc, kc, vc, diff_lambda[layer].astype(jnp.float32), g_c)

        mg_in = [h_res, g1, wts['w_gate'], _row(gate_b[layer]), out_a, out_b, out_c, wts['wb'], wts['wo']]
        mg_specs = [tok(d), _full(g1.shape), _full(wts['w_gate'].shape), _full((1, N_BRANCHES * D_MODEL)),
                    tok(MLA_OUT), tok(SB_OUT), tok(DIFF_OUT), _full(wts['wb'].shape), _full(wts['wo'].shape)]
        h_res = pl.pallas_call(
            _merge_kernel, grid=(b, n_tm), in_specs=mg_specs, out_specs=tok(d),
            out_shape=jax.ShapeDtypeStruct((b, p_len, d), jnp.float32),
            compiler_params=cparams2, name="merge_out")(*mg_in)

        g2 = _row(ln2_g[layer])
        h_res = pl.pallas_call(
            functools.partial(_ffn_kernel, n_chunks=4), grid=(b, n_tm),
            in_specs=[tok(d), _full(g2.shape), _full(wts['w1'].shape), _full(wts['w2'].shape)],
            out_specs=tok(d), out_shape=jax.ShapeDtypeStruct((b, p_len, d), jnp.float32),
            compiler_params=cparams2, name="ffn")(h_res, g2, wts['w1'], wts['w2'])

    return h_res[:, PAD + N_META:]
```

```python
import functools
import math

import jax
import jax.numpy as jnp
from jax import lax
from jax.experimental import pallas as pl
from jax.experimental.pallas import tpu as pltpu

D_MODEL = 1024
DEPTH = 2
N_META = 16
BLOCK = 128
PAD = BLOCK - N_META
ROPE_THETA = 10000.0
NORM_EPS = 1e-6
MASK_VALUE = -1e30

MLA_HEADS = 8
MLA_Q_LORA = 256
MLA_KV_LORA = 128
MLA_NOPE = 64
MLA_ROPE = 32
MLA_V = 64
MLA_QK = MLA_NOPE + MLA_ROPE
SB_HEADS = 8
SB_HEAD_DIM = 64
DIFF_HEADS = 4
DIFF_HEAD_DIM = 64
DIFF_V_DIM = 2 * DIFF_HEAD_DIM
N_BRANCHES = 3
MLA_OUT = MLA_HEADS * MLA_V
SB_OUT = SB_HEADS * SB_HEAD_DIM
DIFF_QK = DIFF_HEADS * 2 * DIFF_HEAD_DIM
DIFF_OUT = DIFF_HEADS * DIFF_V_DIM
D_FF = 4 * D_MODEL
IN_SIZES = (MLA_Q_LORA, MLA_KV_LORA, MLA_ROPE, SB_OUT, SB_OUT, SB_OUT,
            DIFF_QK, DIFF_QK, DIFF_OUT, N_BRANCHES * D_MODEL)

LANES = 128
HALF_LANES = LANES // 2
MXU_DIM = 256
CHAIN = 128

LOG2E = math.log2(math.e)
VMEM_LIMIT_BYTES = 56 * 1024 * 1024
STICK_SKIP_LOG2 = 110.0 * LOG2E

_NT = (((1,), (1,)), ((), ()))


def _tiles(p_len):
    assert p_len % 528 == 0 and (p_len - BLOCK) % MXU_DIM == 0, p_len
    return dict(tm=528, t_soft=MXU_DIM, t_sb=BLOCK)


def _rms(x, g, n):
    ss = jnp.sum(x * x, axis=-1, keepdims=True) * (1.0 / n)
    return (x * lax.rsqrt(ss + NORM_EPS)) * g


def _rope_lanes(x, cos, sina, sinb, half):
    return x * cos + pltpu.roll(x, LANES - half, 1) * sina + pltpu.roll(x, half, 1) * sinb


def _bdot(a, b):
    return jnp.dot(a, b, preferred_element_type=jnp.float32)


def _dot_nt(a, b):
    return lax.dot_general(a, b, _NT, preferred_element_type=jnp.float32)


def _pre_kernel(x_ref, g1_ref, wlat_ref, gcq_ref, gckv_ref, wuq_ref, wk_ref, wv_ref,
                gq_ref, gk_ref, cosa_ref, sinaa_ref, sinba_ref,
                wsb_ref, wdf_ref, gdq_ref, gdk_ref, cosc_ref, sinac_ref, sinbc_ref,
                qa_ref, ka_ref, va_ref, qb_ref, kb_ref, vb_ref, qc_ref, kc_ref, vc_ref):
    bf = jnp.bfloat16
    x = x_ref[...]
    h = _rms(x, g1_ref[...], D_MODEL).astype(bf)

    lat = _bdot(h, wlat_ref[...])
    cq = _rms(lat[:, :MLA_Q_LORA], gcq_ref[...], MLA_Q_LORA).astype(bf)
    ckv = _rms(lat[:, MLA_Q_LORA:MLA_Q_LORA + MLA_KV_LORA], gckv_ref[...], MLA_KV_LORA).astype(bf)
    kr_block = lat[:, MLA_Q_LORA + MLA_KV_LORA:]
    qf = _bdot(cq, wuq_ref[...])
    kf = _bdot(ckv, wk_ref[...])
    va_ref[...] = _bdot(ckv, wv_ref[...]).astype(bf)
    cosa, sinaa, sinba = cosa_ref[...], sinaa_ref[...], sinba_ref[...]
    gq, gk = gq_ref[...], gk_ref[...]
    q_scale = (MLA_QK ** -0.5) * LOG2E
    for hh in range(MLA_HEADS):
        sl = slice(hh * LANES, (hh + 1) * LANES)
        qh = _rope_lanes(_rms(qf[:, sl], gq, MLA_QK), cosa, sinaa, sinba, MLA_ROPE // 2)
        qa_ref[:, sl] = (qh * q_scale).astype(bf)
        kh = _rope_lanes(_rms(kf[:, sl] + kr_block, gk, MLA_QK), cosa, sinaa, sinba, MLA_ROPE // 2)
        ka_ref[:, sl] = kh.astype(bf)

    sb = _bdot(h, wsb_ref[...])
    qb_ref[...] = (sb[:, :SB_OUT] * ((SB_HEAD_DIM ** -0.5) * LOG2E)).astype(bf)
    kb_ref[...] = sb[:, SB_OUT:2 * SB_OUT].astype(bf)
    vb_ref[...] = sb[:, 2 * SB_OUT:].astype(bf)

    df = _bdot(h, wdf_ref[...])
    cosc, sinac, sinbc = cosc_ref[...], sinac_ref[...], sinbc_ref[...]
    lane = lax.broadcasted_iota(jnp.int32, (1, LANES), 1)
    lo = lane < HALF_LANES
    c_scale = (DIFF_HEAD_DIM ** -0.5) * LOG2E

    def half_norm(t, g):
        sq = t * t
        ss_lo = jnp.sum(jnp.where(lo, sq, 0.0), axis=-1, keepdims=True)
        ss_hi = jnp.sum(jnp.where(lo, 0.0, sq), axis=-1, keepdims=True)
        r = jnp.where(lo, lax.rsqrt(ss_lo * (1.0 / DIFF_HEAD_DIM) + NORM_EPS),
                      lax.rsqrt(ss_hi * (1.0 / DIFF_HEAD_DIM) + NORM_EPS))
        return (t * r) * g

    for hh in range(DIFF_HEADS):
        sl = slice(hh * LANES, (hh + 1) * LANES)
        qh = _rope_lanes(half_norm(df[:, sl], gdq_ref[...]), cosc, sinac, sinbc, DIFF_HEAD_DIM // 2)
        qc_ref[:, sl] = (qh * c_scale).astype(bf)
        sk = slice(DIFF_QK + hh * LANES, DIFF_QK + (hh + 1) * LANES)
        kh = _rope_lanes(half_norm(df[:, sk], gdk_ref[...]), cosc, sinac, sinbc, DIFF_HEAD_DIM // 2)
        kc_ref[:, sl] = kh.astype(bf)
    vc_ref[...] = df[:, 2 * DIFF_QK:].astype(bf)


def _softmax2_kernel(*refs, mode, p_len, tile, lam_init=None):
    if mode == 'mla':
        q_ref, k_ref, v_ref, o_ref, m_scr, l_scr, acc_scr, s0_scr, s1_scr = refs
    else:
        q_ref, k_ref, v_ref, lam_ref, g_ref, o_ref, m_scr, l_scr, acc_scr, s0_scr, s1_scr = refs
    bf = jnp.bfloat16
    n_tiles = (p_len - BLOCK) // tile
    n_half = tile // CHAIN
    n_chain = 2 * n_half
    lane = lax.broadcasted_iota(jnp.int32, (1, LANES), 1)
    lo = lane < HALF_LANES

    if mode == 'diff':
        lp = lam_ref[...]
        lam = (jnp.exp(jnp.sum(lp[0:1] * lp[1:2], axis=-1, keepdims=True))
               - jnp.exp(jnp.sum(lp[2:3] * lp[3:4], axis=-1, keepdims=True)) + lam_init)

    def load_q(r0, n):
        rows = pl.ds(r0, n)
        if mode == 'mla':
            return q_ref[rows, :LANES], q_ref[rows, LANES:]
        q = q_ref[rows, :]
        zero = jnp.zeros_like(q)
        return jnp.where(lo, q, zero), jnp.where(lo, zero, q)

    def k_block(r0, n, m):
        rows = pl.ds(r0, n)
        return k_ref[rows, m * LANES:(m + 1) * LANES] if mode == 'mla' else k_ref[rows, :]

    def combine(o0, o1):
        if mode == 'mla':
            return jnp.where(lo, o0, o1)
        return _rms(o0 - lam * o1, g_ref[...], DIFF_V_DIM) * (1.0 - lam_init)

    row = lax.broadcasted_iota(jnp.int32, (CHAIN, 1), 0)
    col0 = lax.broadcasted_iota(jnp.int32, (1, BLOCK), 1)
    meta_ok = col0 >= PAD

    q_meta = load_q(0, BLOCK)
    v_meta = v_ref[pl.ds(0, BLOCK), :]
    o_meta = []
    for m in range(2):
        s = jnp.where((col0 <= row) & meta_ok, _dot_nt(q_meta[m], k_block(0, BLOCK, m)), MASK_VALUE)
        p = jnp.exp2(s - jnp.max(s, axis=-1, keepdims=True))
        o_meta.append(_bdot(p.astype(bf), v_meta) / jnp.sum(p, axis=-1, keepdims=True))
    o_ref[pl.ds(0, BLOCK), :] = combine(*o_meta).astype(bf)

    col = lax.broadcasted_iota(jnp.int32, (1, tile), 1)
    causal = [col <= row + (c % n_half) * CHAIN for c in range(n_chain)]

    def key_row(j):
        return pl.multiple_of(BLOCK + j * tile, BLOCK)

    def qk_into(s_scr, qs, r0):
        for m in range(2):
            s_scr[m * tile:(m + 1) * tile, :] = _dot_nt(qs[m], k_block(r0, tile, m))

    def update(get_s, r0, n, masks):
        vblk = v_ref[pl.ds(r0, n), :]
        for c in range(n_chain):
            rs = slice(c * CHAIN, (c + 1) * CHAIN)
            s = get_s(c)
            if masks is not None:
                s = jnp.where(masks[c], s, MASK_VALUE)
            m_run = m_scr[rs, :]
            m_new = jnp.maximum(m_run, jnp.max(s, axis=-1, keepdims=True))
            alpha = jnp.exp2(m_run - m_new)
            ps = [jnp.exp2(s[:, k * LANES:(k + 1) * LANES] - m_new) for k in range(n // LANES)]
            p_sum = ps[0]
            for pk in ps[1:]:
                p_sum = p_sum + pk
            m_scr[rs, :] = m_new
            l_scr[rs, :] = alpha * l_scr[rs, :] + jnp.sum(p_sum, axis=-1, keepdims=True)
            p = ps[0].astype(bf) if len(ps) == 1 else jnp.concatenate([pk.astype(bf) for pk in ps], axis=1)
            acc_scr[rs, :] = alpha * acc_scr[rs, :] + _bdot(p, vblk)

    def from_scratch(s_scr):
        return lambda c: s_scr[c * CHAIN:(c + 1) * CHAIN, :]

    def q_tile(i, carry):
        r0 = key_row(i)
        qs = load_q(r0, tile)
        m_scr[...] = jnp.full(m_scr.shape, MASK_VALUE, jnp.float32)
        l_scr[...] = jnp.zeros(l_scr.shape, jnp.float32)
        acc_scr[...] = jnp.zeros(acc_scr.shape, jnp.float32)
        qk_into(s0_scr, qs, key_row(0))
        s_meta = [_dot_nt(qs[m], k_block(0, BLOCK, m)) for m in range(2)]
        update(lambda c: s_meta[c // n_half][(c % n_half) * CHAIN:(c % n_half + 1) * CHAIN], 0, BLOCK,
               [meta_ok] * n_chain)

        def pair(t, c_):
            j = 2 * t
            qk_into(s1_scr, qs, key_row(j + 1))
            update(from_scratch(s0_scr), key_row(j), tile, None)
            qk_into(s0_scr, qs, key_row(j + 2))
            update(from_scratch(s1_scr), key_row(j + 1), tile, None)
            return c_

        lax.fori_loop(0, lax.shift_right_logical(i, 1), pair, 0)
        odd = (i & 1) == 1

        @pl.when(odd)
        def _():
            qk_into(s1_scr, qs, r0)
            update(from_scratch(s0_scr), key_row(i - 1), tile, None)
            update(from_scratch(s1_scr), r0, tile, causal)

        @pl.when(jnp.logical_not(odd))
        def _():
            update(from_scratch(s0_scr), r0, tile, causal)

        for h in range(n_half):
            ra = slice(h * CHAIN, (h + 1) * CHAIN)
            rb = slice((n_half + h) * CHAIN, (n_half + h + 1) * CHAIN)
            res = combine(acc_scr[ra, :] / l_scr[ra, :], acc_scr[rb, :] / l_scr[rb, :])
            o_ref[pl.ds(r0 + h * CHAIN, CHAIN), :] = res.astype(bf)
        return carry

    lax.fori_loop(0, n_tiles, q_tile, 0)


def _stick_kernel(q_ref, k_ref, v_ref, tri_ref, o_ref, carry_scr, acc_scr, *, tq, tk):
    bf = jnp.bfloat16
    i = pl.program_id(1)
    n_blocks = q_ref.shape[-1] // LANES
    lane = lax.broadcasted_iota(jnp.int32, (1, LANES), 1)
    lo = lane < HALF_LANES
    qs = []
    for blk in range(n_blocks):
        q = q_ref[:, blk * LANES:(blk + 1) * LANES]
        zero = jnp.zeros_like(q)
        qs.append(jnp.concatenate([jnp.where(lo, q, zero), jnp.where(lo, zero, q)], axis=0))
    tri2 = tri_ref[...]
    qpos = i * tq + (lax.broadcasted_iota(jnp.int32, (2 * tq, 1), 0) & (tq - 1))
    carry_scr[...] = jnp.zeros(carry_scr.shape, jnp.float32)
    acc_scr[...] = jnp.zeros(acc_scr.shape, jnp.float32)

    def block_step(j, masked):
        rows = pl.ds(pl.multiple_of(j * tk, tk), tk)
        if masked:
            kpos = j * tk + lax.broadcasted_iota(jnp.int32, (1, tk), 1)
            mask = (kpos < qpos) & (kpos >= PAD)
        zs = [_dot_nt(qs[blk], k_ref[rows, blk * LANES:(blk + 1) * LANES]) for blk in range(n_blocks)]
        sums, worst = [], None
        for blk in range(n_blocks):
            z = zs[blk]
            keep = -(jnp.maximum(z, 0.0) + jnp.log2(1.0 + jnp.exp2(-jnp.abs(z))))
            if masked:
                keep = jnp.where(mask, keep, 0.0)
            hi = keep.astype(bf)
            lw = (keep - hi.astype(jnp.float32)).astype(bf)
            rs = slice(2 * blk * tq, 2 * (blk + 1) * tq)
            carry = carry_scr[rs, :]
            sums.append(carry + _bdot(jnp.concatenate([hi, lw], axis=1), tri2))
            carry = carry + jnp.sum(keep, axis=-1, keepdims=True)
            carry_scr[rs, :] = carry
            worst = carry if worst is None else jnp.maximum(worst, carry)
        for blk in range(n_blocks):
            a = jnp.exp2(zs[blk] + sums[blk])
            if masked:
                a = jnp.where(mask, a, 0.0)
            rs = slice(2 * blk * tq, 2 * (blk + 1) * tq)
            acc_scr[rs, :] += _bdot(a.astype(bf), v_ref[rows, blk * LANES:(blk + 1) * LANES])
        return jnp.max(worst)

    def cond(st):
        j, worst = st
        return (j >= 0) & (worst > -STICK_SKIP_LOG2)

    def body(st):
        j, _ = st
        worst = lax.cond((j == i) | (j == 0), lambda: block_step(j, True), lambda: block_step(j, False))
        return j - 1, worst

    lax.while_loop(cond, body, (i, jnp.float32(0.0)))
    for blk in range(n_blocks):
        r0 = 2 * blk * tq
        o_ref[:, blk * LANES:(blk + 1) * LANES] = jnp.where(
            lo, acc_scr[r0:r0 + tq, :], acc_scr[r0 + tq:r0 + 2 * tq, :]).astype(bf)


def _merge_kernel(x_ref, g1_ref, wg_ref, gb_ref, oa_ref, ob_ref, oc_ref, wb_ref, wo_ref, y_ref):
    bf = jnp.bfloat16
    x = x_ref[...]
    h = _rms(x, g1_ref[...], D_MODEL).astype(bf)
    outs = (oa_ref, ob_ref, oc_ref)
    merged = None
    for br in range(N_BRANCHES):
        cs = slice(br * D_MODEL, (br + 1) * D_MODEL)
        logits = _bdot(h, wg_ref[:, cs]) + gb_ref[:, cs]
        gate = 1.0 / (1.0 + jnp.exp(-logits))
        rs = slice(br * MLA_OUT, (br + 1) * MLA_OUT)
        term = gate * _bdot(outs[br][...], wb_ref[rs, :])
        merged = term if merged is None else merged + term
    y_ref[...] = x + _bdot(merged.astype(bf), wo_ref[...])


def _ffn_kernel(x_ref, g2_ref, w1_ref, w2_ref, y_ref, *, n_chunks):
    bf = jnp.bfloat16
    x = x_ref[...]
    h = _rms(x, g2_ref[...], D_MODEL).astype(bf)
    ck = D_FF // n_chunks
    acc = x
    for c in range(n_chunks):
        cs = slice(c * ck, (c + 1) * ck)
        f = jnp.maximum(_bdot(h, w1_ref[:, cs]), 0.0)
        acc = acc + _bdot((f * f).astype(bf), w2_ref[cs, :])
    y_ref[...] = acc


def _rope_tables(p_len, d, blocks):
    half = d // 2
    pos = jnp.maximum(jnp.arange(p_len) - PAD, 0)
    inv_freq = jnp.exp(-math.log(ROPE_THETA) * (2.0 * jnp.arange(half, dtype=jnp.float32) / d))
    ang = pos.astype(jnp.float32)[:, None] * inv_freq[None, :]
    cos, sin = jnp.cos(ang), jnp.sin(ang)
    cos_t = jnp.ones((p_len, LANES), jnp.float32)
    sina_t = jnp.zeros((p_len, LANES), jnp.float32)
    sinb_t = jnp.zeros((p_len, LANES), jnp.float32)
    for off in blocks:
        cos_t = cos_t.at[:, off:off + half].set(cos).at[:, off + half:off + d].set(cos)
        sina_t = sina_t.at[:, off:off + half].set(-sin)
        sinb_t = sinb_t.at[:, off + half:off + d].set(sin)
    return cos_t, sina_t, sinb_t


def _row(v):
    return v.reshape(1, -1).astype(jnp.float32)


def _full(shape):
    return pl.BlockSpec(shape, lambda *_: (0,) * len(shape))


def _layer_weights(layer, w_in, mla_w_uq, mla_w_ukv, w_branch, w_out, w_ff1, w_ff2):
    bf = jnp.bfloat16
    w = w_in[layer]
    offs = [0]
    for s in IN_SIZES:
        offs.append(offs[-1] + s)
    d = w.shape[0]
    w_lat = jnp.concatenate([
        w[:, offs[0]:offs[2]],
        jnp.zeros((d, MLA_NOPE), w.dtype), w[:, offs[2]:offs[3]],
        jnp.zeros((d, LANES - MLA_QK), w.dtype)], axis=1).astype(bf)
    w_sb = w[:, offs[3]:offs[6]].astype(bf)
    w_df = w[:, offs[6]:offs[9]].astype(bf)
    w_gate = w[:, offs[9]:offs[10]].astype(bf)
    wuq = mla_w_uq[layer].reshape(MLA_Q_LORA, MLA_HEADS, MLA_QK)
    wuq = jnp.pad(wuq, ((0, 0), (0, 0), (0, LANES - MLA_QK))).reshape(MLA_Q_LORA, MLA_HEADS * LANES).astype(bf)
    wukv = mla_w_ukv[layer].reshape(MLA_KV_LORA, MLA_HEADS, MLA_NOPE + MLA_V)
    wk = jnp.pad(wukv[:, :, :MLA_NOPE], ((0, 0), (0, 0), (0, LANES - MLA_NOPE)))
    wk = wk.reshape(MLA_KV_LORA, MLA_HEADS * LANES).astype(bf)
    wv = wukv[:, :, MLA_NOPE:].reshape(MLA_KV_LORA, MLA_OUT).astype(bf)
    return dict(w_lat=w_lat, w_sb=w_sb, w_df=w_df, w_gate=w_gate, wuq=wuq, wk=wk, wv=wv,
                wb=w_branch[layer].astype(bf), wo=w_out[layer].astype(bf),
                w1=w_ff1[layer].astype(bf), w2=w_ff2[layer].astype(bf))


def kernel(x, meta_tokens, ln1_g, w_in, mla_cq_norm_g, mla_ckv_norm_g, mla_w_uq, mla_w_ukv,
           mla_q_norm_g, mla_k_norm_g, diff_q_norm_g, diff_k_norm_g, diff_lambda,
           diff_out_norm_g, gate_b, w_branch, w_out, ln2_g, w_ff1, w_ff2):
    b, seq, d = x.shape
    assert d == D_MODEL
    bf = jnp.bfloat16
    pad = jnp.zeros((b, PAD, d), x.dtype)
    meta = jnp.broadcast_to(meta_tokens.astype(x.dtype)[None], (b, N_META, d))
    h_res = jnp.concatenate([pad, meta, x], axis=1)
    p_len = h_res.shape[1]
    t = _tiles(p_len)
    tm, tqs, tqb = t['tm'], t['t_soft'], t['t_sb']
    n_tm = p_len // tm

    tab_a = _rope_tables(p_len, MLA_ROPE, [MLA_NOPE])
    tab_c = _rope_tables(p_len, DIFF_HEAD_DIM, [0, HALF_LANES])
    tri = (jnp.arange(tqb)[:, None] >= jnp.arange(tqb)[None, :]).astype(bf)
    tri2 = jnp.concatenate([tri, tri], axis=0)

    cparams2 = pltpu.CompilerParams(dimension_semantics=("parallel", "parallel"),
                                    vmem_limit_bytes=VMEM_LIMIT_BYTES)
    cparams2s = pltpu.CompilerParams(dimension_semantics=("parallel", "arbitrary"),
                                     vmem_limit_bytes=VMEM_LIMIT_BYTES)

    def tok(width):
        return pl.BlockSpec((None, tm, width), lambda bi, ti: (bi, ti, 0))

    def tab():
        return pl.BlockSpec((tm, LANES), lambda bi, ti: (ti, 0))

    def act(width):
        return jax.ShapeDtypeStruct((b, p_len, width), bf)

    soft_scratch = [pltpu.VMEM((2 * tqs, LANES), jnp.float32), pltpu.VMEM((2 * tqs, LANES), jnp.float32),
                    pltpu.VMEM((2 * tqs, LANES), jnp.float32),
                    pltpu.VMEM((2 * tqs, tqs), jnp.float32), pltpu.VMEM((2 * tqs, tqs), jnp.float32)]

    def seq_block(width):
        return pl.BlockSpec((None, p_len, width), lambda bi, ci: (bi, 0, ci))

    for layer in range(DEPTH):
        wts = _layer_weights(layer, w_in, mla_w_uq, mla_w_ukv, w_branch, w_out, w_ff1, w_ff2)
        g1 = _row(ln1_g[layer])
        gq = _row(jnp.pad(mla_q_norm_g[layer], (0, LANES - MLA_QK)))
        gk = _row(jnp.pad(mla_k_norm_g[layer], (0, LANES - MLA_QK)))
        gdq = _row(jnp.tile(diff_q_norm_g[layer], 2))
        gdk = _row(jnp.tile(diff_k_norm_g[layer], 2))

        pre_in = [h_res, g1, wts['w_lat'], _row(mla_cq_norm_g[layer]), _row(mla_ckv_norm_g[layer]),
                  wts['wuq'], wts['wk'], wts['wv'], gq, gk, *tab_a,
                  wts['w_sb'], wts['w_df'], gdq, gdk, *tab_c]
        pre_specs = [tok(d)] + [_full(a.shape) for a in pre_in[1:10]] + [tab()] * 3 \
            + [_full(a.shape) for a in pre_in[13:17]] + [tab()] * 3
        widths = (MLA_HEADS * LANES, MLA_HEADS * LANES, MLA_OUT, SB_OUT, SB_OUT, SB_OUT,
                  DIFF_QK, DIFF_QK, DIFF_OUT)
        qa, ka, va, qb, kb, vb, qc, kc, vc = pl.pallas_call(
            _pre_kernel, grid=(b, n_tm), in_specs=pre_specs,
            out_specs=[tok(wd) for wd in widths], out_shape=[act(wd) for wd in widths],
            compiler_params=cparams2, name="pre_tokens")(*pre_in)

        out_a = pl.pallas_call(
            functools.partial(_softmax2_kernel, mode='mla', p_len=p_len, tile=tqs),
            grid=(b, MLA_HEADS // 2),
            in_specs=[seq_block(2 * LANES), seq_block(2 * LANES), seq_block(LANES)],
            out_specs=seq_block(LANES), scratch_shapes=soft_scratch,
            out_shape=act(MLA_OUT), compiler_params=cparams2, name="attn_mla")(qa, ka, va)

        n_qb = p_len // tqb
        out_b = pl.pallas_call(
            functools.partial(_stick_kernel, tq=tqb, tk=tqb),
            grid=(b, n_qb),
            in_specs=[pl.BlockSpec((None, tqb, SB_OUT), lambda bi, qi: (bi, qi, 0)),
                      pl.BlockSpec((None, p_len, SB_OUT), lambda bi, qi: (bi, 0, 0)),
                      pl.BlockSpec((None, p_len, SB_OUT), lambda bi, qi: (bi, 0, 0)),
                      _full(tri2.shape)],
            out_specs=pl.BlockSpec((None, tqb, SB_OUT), lambda bi, qi: (bi, qi, 0)),
            scratch_shapes=[pltpu.VMEM((SB_HEADS * tqb, LANES), jnp.float32)] * 2,
            out_shape=act(SB_OUT), compiler_params=cparams2s, name="attn_stick")(qb, kb, vb, tri2)

        lam_init = 0.8 - 0.6 * math.exp(-0.3 * layer)
        g_c = _row(diff_out_norm_g[layer])
        out_c = pl.pallas_call(
            functools.partial(_softmax2_kernel, mode='diff', p_len=p_len, tile=tqs, lam_init=lam_init),
            grid=(b, DIFF_HEADS),
            in_specs=[seq_block(LANES), seq_block(LANES), seq_block(LANES),
                      _full((4, DIFF_HEAD_DIM)), _full((1, LANES))],
            out_specs=seq_block(LANES), scratch_shapes=soft_scratch,
            out_shape=act(DIFF_OUT), compiler_params=cparams2, name="attn_diff")(
                qc, kc, vc, diff_lambda[layer].astype(jnp.float32), g_c)

        mg_in = [h_res, g1, wts['w_gate'], _row(gate_b[layer]), out_a, out_b, out_c, wts['wb'], wts['wo']]
        mg_specs = [tok(d), _full(g1.shape), _full(wts['w_gate'].shape), _full((1, N_BRANCHES * D_MODEL)),
                    tok(MLA_OUT), tok(SB_OUT), tok(DIFF_OUT), _full(wts['wb'].shape), _full(wts['wo'].shape)]
        h_res = pl.pallas_call(
            _merge_kernel, grid=(b, n_tm), in_specs=mg_specs, out_specs=tok(d),
            out_shape=jax.ShapeDtypeStruct((b, p_len, d), jnp.float32),
            compiler_params=cparams2, name="merge_out")(*mg_in)

        g2 = _row(ln2_g[layer])
        h_res = pl.pallas_call(
            functools.partial(_ffn_kernel, n_chunks=4), grid=(b, n_tm),
            in_specs=[tok(d), _full(g2.shape), _full(wts['w1'].shape), _full(wts['w2'].shape)],
            out_specs=tok(d), out_shape=jax.ShapeDtypeStruct((b, p_len, d), jnp.float32),
            compiler_params=cparams2, name="ffn")(h_res, g2, wts['w1'], wts['w2'])

    return h_res[:, PAD + N_META:]
```

```python
import functools
import math

import jax
import jax.numpy as jnp
from jax import lax
from jax.experimental import pallas as pl
from jax.experimental.pallas import tpu as pltpu

D_MODEL = 1024
DEPTH = 2
N_META = 16
BLOCK = 128
PAD = BLOCK - N_META
ROPE_THETA = 10000.0
NORM_EPS = 1e-6
MASK_VALUE = -1e30

MLA_HEADS = 8
MLA_Q_LORA = 256
MLA_KV_LORA = 128
MLA_NOPE = 64
MLA_ROPE = 32
MLA_V = 64
MLA_QK = MLA_NOPE + MLA_ROPE
SB_HEADS = 8
SB_HEAD_DIM = 64
DIFF_HEADS = 4
DIFF_HEAD_DIM = 64
DIFF_V_DIM = 2 * DIFF_HEAD_DIM
N_BRANCHES = 3
MLA_OUT = MLA_HEADS * MLA_V
SB_OUT = SB_HEADS * SB_HEAD_DIM
DIFF_QK = DIFF_HEADS * 2 * DIFF_HEAD_DIM
DIFF_OUT = DIFF_HEADS * DIFF_V_DIM
D_FF = 4 * D_MODEL
IN_SIZES = (MLA_Q_LORA, MLA_KV_LORA, MLA_ROPE, SB_OUT, SB_OUT, SB_OUT,
            DIFF_QK, DIFF_QK, DIFF_OUT, N_BRANCHES * D_MODEL)

LANES = 128
HALF_LANES = LANES // 2
MXU_DIM = 256
CHAIN = 128

LOG2E = math.log2(math.e)
VMEM_LIMIT_BYTES = 56 * 1024 * 1024
STICK_SKIP_LOG2 = 110.0 * LOG2E

_NT = (((1,), (1,)), ((), ()))


def _tiles(p_len):
    assert p_len % 528 == 0 and (p_len - BLOCK) % MXU_DIM == 0, p_len
    return dict(tm=528, t_soft=MXU_DIM, t_sb=BLOCK)


def _rms(x, g, n):
    ss = jnp.sum(x * x, axis=-1, keepdims=True) * (1.0 / n)
    return (x * lax.rsqrt(ss + NORM_EPS)) * g


def _rope_lanes(x, cos, sina, sinb, half):
    return x * cos + pltpu.roll(x, LANES - half, 1) * sina + pltpu.roll(x, half, 1) * sinb


def _bdot(a, b):
    return jnp.dot(a, b, preferred_element_type=jnp.float32)


def _dot_nt(a, b):
    return lax.dot_general(a, b, _NT, preferred_element_type=jnp.float32)


def _pre_kernel(x_ref, g1_ref, wlat_ref, gcq_ref, gckv_ref, wuq_ref, wk_ref, wv_ref,
                gq_ref, gk_ref, cosa_ref, sinaa_ref, sinba_ref,
                wsb_ref, wdf_ref, gdq_ref, gdk_ref, cosc_ref, sinac_ref, sinbc_ref,
                qa_ref, ka_ref, va_ref, qb_ref, kb_ref, vb_ref, qc_ref, kc_ref, vc_ref):
    bf = jnp.bfloat16
    x = x_ref[...]
    h = _rms(x, g1_ref[...], D_MODEL).astype(bf)

    lat = _bdot(h, wlat_ref[...])
    cq = _rms(lat[:, :MLA_Q_LORA], gcq_ref[...], MLA_Q_LORA).astype(bf)
    ckv = _rms(lat[:, MLA_Q_LORA:MLA_Q_LORA + MLA_KV_LORA], gckv_ref[...], MLA_KV_LORA).astype(bf)
    kr_block = lat[:, MLA_Q_LORA + MLA_KV_LORA:]
    qf = _bdot(cq, wuq_ref[...])
    kf = _bdot(ckv, wk_ref[...])
    va_ref[...] = _bdot(ckv, wv_ref[...]).astype(bf)
    cosa, sinaa, sinba = cosa_ref[...], sinaa_ref[...], sinba_ref[...]
    gq, gk = gq_ref[...], gk_ref[...]
    q_scale = (MLA_QK ** -0.5) * LOG2E
    for hh in range(MLA_HEADS):
        sl = slice(hh * LANES, (hh + 1) * LANES)
        qh = _rope_lanes(_rms(qf[:, sl], gq, MLA_QK), cosa, sinaa, sinba, MLA_ROPE // 2)
        qa_ref[:, sl] = (qh * q_scale).astype(bf)
        kh = _rope_lanes(_rms(kf[:, sl] + kr_block, gk, MLA_QK), cosa, sinaa, sinba, MLA_ROPE // 2)
        ka_ref[:, sl] = kh.astype(bf)

    sb = _bdot(h, wsb_ref[...])
    qb_ref[...] = (sb[:, :SB_OUT] * ((SB_HEAD_DIM ** -0.5) * LOG2E)).astype(bf)
    kb_ref[...] = sb[:, SB_OUT:2 * SB_OUT].astype(bf)
    vb_ref[...] = sb[:, 2 * SB_OUT:].astype(bf)

    df = _bdot(h, wdf_ref[...])
    cosc, sinac, sinbc = cosc_ref[...], sinac_ref[...], sinbc_ref[...]
    lane = lax.broadcasted_iota(jnp.int32, (1, LANES), 1)
    lo = lane < HALF_LANES
    c_scale = (DIFF_HEAD_DIM ** -0.5) * LOG2E

    def half_norm(t, g):
        sq = t * t
        ss_lo = jnp.sum(jnp.where(lo, sq, 0.0), axis=-1, keepdims=True)
        ss_hi = jnp.sum(jnp.where(lo, 0.0, sq), axis=-1, keepdims=True)
        r = jnp.where(lo, lax.rsqrt(ss_lo * (1.0 / DIFF_HEAD_DIM) + NORM_EPS),
                      lax.rsqrt(ss_hi * (1.0 / DIFF_HEAD_DIM) + NORM_EPS))
        return (t * r) * g

    for hh in range(DIFF_HEADS):
        sl = slice(hh * LANES, (hh + 1) * LANES)
        qh = _rope_lanes(half_norm(df[:, sl], gdq_ref[...]), cosc, sinac, sinbc, DIFF_HEAD_DIM // 2)
        qc_ref[:, sl] = (qh * c_scale).astype(bf)
        sk = slice(DIFF_QK + hh * LANES, DIFF_QK + (hh + 1) * LANES)
        kh = _rope_lanes(half_norm(df[:, sk], gdk_ref[...]), cosc, sinac, sinbc, DIFF_HEAD_DIM // 2)
        kc_ref[:, sl] = kh.astype(bf)
    vc_ref[...] = df[:, 2 * DIFF_QK:].astype(bf)


def _softmax2_kernel(*refs, mode, p_len, tile, lam_init=None):
    if mode == 'mla':
        q_ref, k_ref, v_ref, o_ref, m_scr, l_scr, acc_scr, s0_scr, s1_scr, mx0_scr, mx1_scr, vx_scr = refs
    else:
        q_ref, k_ref, v_ref, lam_ref, g_ref, o_ref, m_scr, l_scr, acc_scr, s0_scr, s1_scr, mx0_scr, mx1_scr, vx_scr = refs
    bf = jnp.bfloat16
    n_tiles = (p_len - BLOCK) // tile
    n_half = tile // CHAIN
    n_chain = 2 * n_half
    lane = lax.broadcasted_iota(jnp.int32, (1, LANES), 1)
    lo = lane < HALF_LANES

    if mode == 'diff':
        lp = lam_ref[...]
        lam = (jnp.exp(jnp.sum(lp[0:1] * lp[1:2], axis=-1, keepdims=True))
               - jnp.exp(jnp.sum(lp[2:3] * lp[3:4], axis=-1, keepdims=True)) + lam_init)

    def load_q(r0, n):
        rows = pl.ds(r0, n)
        if mode == 'mla':
            return q_ref[rows, :LANES], q_ref[rows, LANES:]
        q = q_ref[rows, :]
        zero = jnp.zeros_like(q)
        return jnp.where(lo, q, zero), jnp.where(lo, zero, q)

    def k_block(r0, n, m):
        rows = pl.ds(r0, n)
        return k_ref[rows, m * LANES:(m + 1) * LANES] if mode == 'mla' else k_ref[rows, :]

    def combine(o0, o1):
        if mode == 'mla':
            return jnp.where(lo, o0, o1)
        return _rms(o0 - lam * o1, g_ref[...], DIFF_V_DIM) * (1.0 - lam_init)

    row = lax.broadcasted_iota(jnp.int32, (CHAIN, 1), 0)
    col0 = lax.broadcasted_iota(jnp.int32, (1, BLOCK), 1)
    meta_ok = col0 >= PAD

    q_meta = load_q(0, BLOCK)
    v_meta = v_ref[pl.ds(0, BLOCK), :]
    o_meta = []
    for m in range(2):
        s = jnp.where((col0 <= row) & meta_ok, _dot_nt(q_meta[m], k_block(0, BLOCK, m)), MASK_VALUE)
        p = jnp.exp2(s - jnp.max(s, axis=-1, keepdims=True))
        o_meta.append(_bdot(p.astype(bf), v_meta) / jnp.sum(p, axis=-1, keepdims=True))
    o_ref[pl.ds(0, BLOCK), :] = combine(*o_meta).astype(bf)

    col = lax.broadcasted_iota(jnp.int32, (1, tile), 1)
    causal = [col <= row + (c % n_half) * CHAIN for c in range(n_chain)]

    def key_row(j):
        return pl.multiple_of(BLOCK + j * tile, BLOCK)

    def qk_into(bufs, qs, r0, masks):
        s_scr, mx_scr = bufs
        for m in range(2):
            s = _dot_nt(qs[m], k_block(r0, tile, m))
            for h in range(n_half):
                c = m * n_half + h
                rs = slice(c * CHAIN, (c + 1) * CHAIN)
                s_c = s[h * CHAIN:(h + 1) * CHAIN]
                if masks is not None:
                    s_c = jnp.where(masks[c], s_c, MASK_VALUE)
                s_scr[rs, :] = s_c
                mx_scr[rs, :] = jnp.broadcast_to(jnp.max(s_c, axis=-1, keepdims=True), (CHAIN, LANES))

    def v_ext(vblk, m):
        zero = jnp.zeros_like(vblk)
        if mode == 'diff':
            return jnp.concatenate([vblk, zero] if m == 0 else [zero, vblk], axis=1)
        mine = lo if m == 0 else jnp.logical_not(lo)
        ones = jnp.broadcast_to(jnp.where(mine, 1.0, 0.0).astype(bf), vblk.shape)
        return jnp.concatenate([jnp.where(mine, vblk, zero), ones], axis=1)

    vx_meta_base = n_tiles * 2 * tile
    for m in range(2):
        vx_scr[pl.ds(vx_meta_base + m * BLOCK, BLOCK), :] = v_ext(v_meta, m)

    def build_vx(j, c_):
        vblk = v_ref[pl.ds(key_row(j), tile), :]
        base = pl.multiple_of(j * 2 * tile, 2 * tile)
        vx_scr[pl.ds(base, tile), :] = v_ext(vblk, 0)
        vx_scr[pl.ds(base + tile, tile), :] = v_ext(vblk, 1)
        return c_

    lax.fori_loop(0, n_tiles, build_vx, 0)

    def update(get_s, get_max, vx_base, n):
        rhs = vx_scr[pl.ds(vx_base, 2 * n), :]
        for h in range(n_half):
            parts, alphas = [], []
            for m in range(2):
                c = m * n_half + h
                rs = slice(c * CHAIN, (c + 1) * CHAIN)
                s = get_s(c)
                m_run = m_scr[rs, :]
                m_new = jnp.maximum(m_run, get_max(c))
                alpha = jnp.exp2(m_run - m_new)
                m_scr[rs, :] = m_new
                ps = [jnp.exp2(s[:, k * LANES:(k + 1) * LANES] - m_new) for k in range(n // LANES)]
                if mode == 'diff':
                    p_sum = ps[0]
                    for pk in ps[1:]:
                        p_sum = p_sum + pk
                    l_scr[rs, :] = alpha * l_scr[rs, :] + jnp.sum(p_sum, axis=-1, keepdims=True)
                parts += [pk.astype(bf) for pk in ps]
                alphas.append(alpha)
            pv = _bdot(jnp.concatenate(parts, axis=1), rhs)
            if mode == 'mla':
                both = jnp.where(lo, alphas[0], alphas[1])
                alpha_cat = jnp.concatenate([both, both], axis=1)
            else:
                alpha_cat = jnp.concatenate(alphas, axis=1)
            ah = slice(h * CHAIN, (h + 1) * CHAIN)
            acc_scr[ah, :] = alpha_cat * acc_scr[ah, :] + pv

    def update_from(bufs, vx_base):
        s_scr, mx_scr = bufs
        update(lambda c: s_scr[c * CHAIN:(c + 1) * CHAIN, :], lambda c: mx_scr[c * CHAIN:(c + 1) * CHAIN, :],
               vx_base, tile)

    def vx_row(j):
        return pl.multiple_of(j * 2 * tile, 2 * tile)

    buf0, buf1 = (s0_scr, mx0_scr), (s1_scr, mx1_scr)

    def q_tile(i, carry):
        r0 = key_row(i)
        qs = load_q(r0, tile)
        m_scr[...] = jnp.full(m_scr.shape, MASK_VALUE, jnp.float32)
        if mode == 'diff':
            l_scr[...] = jnp.zeros(l_scr.shape, jnp.float32)
        acc_scr[...] = jnp.zeros(acc_scr.shape, jnp.float32)
        first_is_diag = i == 0

        @pl.when(first_is_diag)
        def _():
            qk_into(buf0, qs, r0, causal)

        @pl.when(jnp.logical_not(first_is_diag))
        def _():
            qk_into(buf0, qs, key_row(0), None)

        s_meta = [jnp.where(meta_ok, _dot_nt(qs[m], k_block(0, BLOCK, m)), MASK_VALUE) for m in range(2)]

        def meta_s(c):
            return s_meta[c // n_half][(c % n_half) * CHAIN:(c % n_half + 1) * CHAIN]

        update(meta_s, lambda c: jnp.max(meta_s(c), axis=-1, keepdims=True), vx_meta_base, BLOCK)

        n_pairs = lax.shift_right_logical(jnp.maximum(i - 1, 0), 1)
        odd = (i & 1) == 1

        def pair(t, c_):
            j = 2 * t
            qk_into(buf1, qs, key_row(j + 1), None)
            update_from(buf0, vx_row(j))
            qk_into(buf0, qs, key_row(j + 2), None)
            update_from(buf1, vx_row(j + 1))
            return c_

        lax.fori_loop(0, n_pairs, pair, 0)

        @pl.when(odd)
        def _():
            qk_into(buf1, qs, r0, causal)
            update_from(buf0, vx_row(i - 1))
            update_from(buf1, vx_row(i))

        @pl.when(jnp.logical_and(jnp.logical_not(odd), i > 0))
        def _():
            qk_into(buf1, qs, key_row(i - 1), None)
            update_from(buf0, vx_row(i - 2))
            qk_into(buf0, qs, r0, causal)
            update_from(buf1, vx_row(i - 1))
            update_from(buf0, vx_row(i))

        @pl.when(first_is_diag)
        def _():
            update_from(buf0, vx_row(i))

        for h in range(n_half):
            a = acc_scr[h * CHAIN:(h + 1) * CHAIN, :]
            if mode == 'mla':
                res = a[:, :LANES] / a[:, LANES:]
            else:
                l0 = l_scr[h * CHAIN:(h + 1) * CHAIN, :]
                l1 = l_scr[(n_half + h) * CHAIN:(n_half + h + 1) * CHAIN, :]
                res = combine(a[:, :LANES] / l0, a[:, LANES:] / l1)
            o_ref[pl.ds(r0 + h * CHAIN, CHAIN), :] = res.astype(bf)
        return carry

    lax.fori_loop(0, n_tiles, q_tile, 0)


def _stick_kernel(q_ref, k_ref, v_ref, tri_ref, o_ref, carry_scr, acc_scr, *, tq, tk):
    bf = jnp.bfloat16
    i = pl.program_id(1)
    n_blocks = q_ref.shape[-1] // LANES
    lane = lax.broadcasted_iota(jnp.int32, (1, LANES), 1)
    lo = lane < HALF_LANES
    qs = []
    for blk in range(n_blocks):
        q = q_ref[:, blk * LANES:(blk + 1) * LANES]
        zero = jnp.zeros_like(q)
        qs.append(jnp.concatenate([jnp.where(lo, q, zero), jnp.where(lo, zero, q)], axis=0))
    tri2 = tri_ref[...]
    qpos = i * tq + (lax.broadcasted_iota(jnp.int32, (2 * tq, 1), 0) & (tq - 1))
    carry_scr[...] = jnp.zeros(carry_scr.shape, jnp.float32)
    acc_scr[...] = jnp.zeros(acc_scr.shape, jnp.float32)

    def block_step(j, masked):
        rows = pl.ds(pl.multiple_of(j * tk, tk), tk)
        if masked:
            kpos = j * tk + lax.broadcasted_iota(jnp.int32, (1, tk), 1)
            mask = (kpos < qpos) & (kpos >= PAD)
        zs = [_dot_nt(qs[blk], k_ref[rows, blk * LANES:(blk + 1) * LANES]) for blk in range(n_blocks)]
        sums, worst = [], None
        for blk in range(n_blocks):
            z = zs[blk]
            keep = -(jnp.maximum(z, 0.0) + jnp.log2(1.0 + jnp.exp2(-jnp.abs(z))))
            if masked:
                keep = jnp.where(mask, keep, 0.0)
            hi = keep.astype(bf)
            lw = (keep - hi.astype(jnp.float32)).astype(bf)
            rs = slice(2 * blk * tq, 2 * (blk + 1) * tq)
            carry = carry_scr[rs, :]
            sums.append(carry + _bdot(jnp.concatenate([hi, lw], axis=1), tri2))
            carry = carry + jnp.sum(keep, axis=-1, keepdims=True)
            carry_scr[rs, :] = carry
            worst = carry if worst is None else jnp.maximum(worst, carry)
        for blk in range(n_blocks):
            a = jnp.exp2(zs[blk] + sums[blk])
            if masked:
                a = jnp.where(mask, a, 0.0)
            rs = slice(2 * blk * tq, 2 * (blk + 1) * tq)
            acc_scr[rs, :] += _bdot(a.astype(bf), v_ref[rows, blk * LANES:(blk + 1) * LANES])
        return jnp.max(worst)

    def cond(st):
        j, worst = st
        return (j >= 0) & (worst > -STICK_SKIP_LOG2)

    def body(st):
        j, _ = st
        worst = lax.cond((j == i) | (j == 0), lambda: block_step(j, True), lambda: block_step(j, False))
        return j - 1, worst

    lax.while_loop(cond, body, (i, jnp.float32(0.0)))
    for blk in range(n_blocks):
        r0 = 2 * blk * tq
        o_ref[:, blk * LANES:(blk + 1) * LANES] = jnp.where(
            lo, acc_scr[r0:r0 + tq, :], acc_scr[r0 + tq:r0 + 2 * tq, :]).astype(bf)


def _merge_kernel(x_ref, g1_ref, wg_ref, gb_ref, oa_ref, ob_ref, oc_ref, wb_ref, wo_ref, y_ref):
    bf = jnp.bfloat16
    x = x_ref[...]
    h = _rms(x, g1_ref[...], D_MODEL).astype(bf)
    outs = (oa_ref, ob_ref, oc_ref)
    merged = None
    for br in range(N_BRANCHES):
        cs = slice(br * D_MODEL, (br + 1) * D_MODEL)
        logits = _bdot(h, wg_ref[:, cs]) + gb_ref[:, cs]
        gate = 1.0 / (1.0 + jnp.exp(-logits))
        rs = slice(br * MLA_OUT, (br + 1) * MLA_OUT)
        term = gate * _bdot(outs[br][...], wb_ref[rs, :])
        merged = term if merged is None else merged + term
    y_ref[...] = x + _bdot(merged.astype(bf), wo_ref[...])


def _ffn_kernel(x_ref, g2_ref, w1_ref, w2_ref, y_ref, *, n_chunks):
    bf = jnp.bfloat16
    x = x_ref[...]
    h = _rms(x, g2_ref[...], D_MODEL).astype(bf)
    ck = D_FF // n_chunks
    acc = x
    for c in range(n_chunks):
        cs = slice(c * ck, (c + 1) * ck)
        f = jnp.maximum(_bdot(h, w1_ref[:, cs]), 0.0)
        acc = acc + _bdot((f * f).astype(bf), w2_ref[cs, :])
    y_ref[...] = acc


def _rope_tables(p_len, d, blocks):
    half = d // 2
    pos = jnp.maximum(jnp.arange(p_len) - PAD, 0)
    inv_freq = jnp.exp(-math.log(ROPE_THETA) * (2.0 * jnp.arange(half, dtype=jnp.float32) / d))
    ang = pos.astype(jnp.float32)[:, None] * inv_freq[None, :]
    cos, sin = jnp.cos(ang), jnp.sin(ang)
    zero = jnp.zeros_like(sin)
    cos_p, sina_p, sinb_p, at = [], [], [], 0
    for off in blocks:
        gap = off - at
        cos_p += [jnp.ones((p_len, gap), jnp.float32), cos, cos]
        sina_p += [jnp.zeros((p_len, gap), jnp.float32), -sin, zero]
        sinb_p += [jnp.zeros((p_len, gap), jnp.float32), zero, sin]
        at = off + d
    cos_p.append(jnp.ones((p_len, LANES - at), jnp.float32))
    sina_p.append(jnp.zeros((p_len, LANES - at), jnp.float32))
    sinb_p.append(jnp.zeros((p_len, LANES - at), jnp.float32))
    return tuple(jnp.concatenate(parts, axis=1) for parts in (cos_p, sina_p, sinb_p))


def _row(v):
    return v.reshape(1, -1).astype(jnp.float32)


def _full(shape):
    return pl.BlockSpec(shape, lambda *_: (0,) * len(shape))


def _layer_weights(layer, w_in, mla_w_uq, mla_w_ukv, w_branch, w_out, w_ff1, w_ff2):
    bf = jnp.bfloat16
    w = w_in[layer]
    offs = [0]
    for s in IN_SIZES:
        offs.append(offs[-1] + s)
    d = w.shape[0]
    w_lat = jnp.concatenate([
        w[:, offs[0]:offs[2]],
        jnp.zeros((d, MLA_NOPE), w.dtype), w[:, offs[2]:offs[3]],
        jnp.zeros((d, LANES - MLA_QK), w.dtype)], axis=1).astype(bf)
    w_sb = w[:, offs[3]:offs[6]].astype(bf)
    w_df = w[:, offs[6]:offs[9]].astype(bf)
    w_gate = w[:, offs[9]:offs[10]].astype(bf)
    wuq = mla_w_uq[layer].reshape(MLA_Q_LORA, MLA_HEADS, MLA_QK)
    wuq = jnp.pad(wuq, ((0, 0), (0, 0), (0, LANES - MLA_QK))).reshape(MLA_Q_LORA, MLA_HEADS * LANES).astype(bf)
    wukv = mla_w_ukv[layer].reshape(MLA_KV_LORA, MLA_HEADS, MLA_NOPE + MLA_V)
    wk = jnp.pad(wukv[:, :, :MLA_NOPE], ((0, 0), (0, 0), (0, LANES - MLA_NOPE)))
    wk = wk.reshape(MLA_KV_LORA, MLA_HEADS * LANES).astype(bf)
    wv = wukv[:, :, MLA_NOPE:].reshape(MLA_KV_LORA, MLA_OUT).astype(bf)
    return dict(w_lat=w_lat, w_sb=w_sb, w_df=w_df, w_gate=w_gate, wuq=wuq, wk=wk, wv=wv,
                wb=w_branch[layer].astype(bf), wo=w_out[layer].astype(bf),
                w1=w_ff1[layer].astype(bf), w2=w_ff2[layer].astype(bf))


def kernel(x, meta_tokens, ln1_g, w_in, mla_cq_norm_g, mla_ckv_norm_g, mla_w_uq, mla_w_ukv,
           mla_q_norm_g, mla_k_norm_g, diff_q_norm_g, diff_k_norm_g, diff_lambda,
           diff_out_norm_g, gate_b, w_branch, w_out, ln2_g, w_ff1, w_ff2):
    b, seq, d = x.shape
    assert d == D_MODEL
    bf = jnp.bfloat16
    pad = jnp.zeros((b, PAD, d), x.dtype)
    meta = jnp.broadcast_to(meta_tokens.astype(x.dtype)[None], (b, N_META, d))
    h_res = jnp.concatenate([pad, meta, x], axis=1)
    p_len = h_res.shape[1]
    t = _tiles(p_len)
    tm, tqs, tqb = t['tm'], t['t_soft'], t['t_sb']
    n_tm = p_len // tm

    tab_a = _rope_tables(p_len, MLA_ROPE, [MLA_NOPE])
    tab_c = _rope_tables(p_len, DIFF_HEAD_DIM, [0, HALF_LANES])
    tri = (jnp.arange(tqb)[:, None] >= jnp.arange(tqb)[None, :]).astype(bf)
    tri2 = jnp.concatenate([tri, tri], axis=0)

    cparams2 = pltpu.CompilerParams(dimension_semantics=("parallel", "parallel"),
                                    vmem_limit_bytes=VMEM_LIMIT_BYTES)
    cparams2s = pltpu.CompilerParams(dimension_semantics=("parallel", "arbitrary"),
                                     vmem_limit_bytes=VMEM_LIMIT_BYTES)

    def tok(width):
        return pl.BlockSpec((None, tm, width), lambda bi, ti: (bi, ti, 0))

    def tab():
        return pl.BlockSpec((tm, LANES), lambda bi, ti: (ti, 0))

    def act(width):
        return jax.ShapeDtypeStruct((b, p_len, width), bf)

    soft_scratch = [pltpu.VMEM((2 * tqs, LANES), jnp.float32), pltpu.VMEM((2 * tqs, LANES), jnp.float32),
                    pltpu.VMEM((tqs, 2 * LANES), jnp.float32),
                    pltpu.VMEM((2 * tqs, tqs), jnp.float32), pltpu.VMEM((2 * tqs, tqs), jnp.float32),
                    pltpu.VMEM((2 * tqs, LANES), jnp.float32), pltpu.VMEM((2 * tqs, LANES), jnp.float32),
                    pltpu.VMEM((2 * (p_len - BLOCK) + 2 * BLOCK, 2 * LANES), bf)]

    def seq_block(width):
        return pl.BlockSpec((None, p_len, width), lambda bi, ci: (bi, 0, ci))

    for layer in range(DEPTH):
        wts = _layer_weights(layer, w_in, mla_w_uq, mla_w_ukv, w_branch, w_out, w_ff1, w_ff2)
        g1 = _row(ln1_g[layer])
        gq = _row(jnp.pad(mla_q_norm_g[layer], (0, LANES - MLA_QK)))
        gk = _row(jnp.pad(mla_k_norm_g[layer], (0, LANES - MLA_QK)))
        gdq = _row(jnp.tile(diff_q_norm_g[layer], 2))
        gdk = _row(jnp.tile(diff_k_norm_g[layer], 2))

        pre_in = [h_res, g1, wts['w_lat'], _row(mla_cq_norm_g[layer]), _row(mla_ckv_norm_g[layer]),
                  wts['wuq'], wts['wk'], wts['wv'], gq, gk, *tab_a,
                  wts['w_sb'], wts['w_df'], gdq, gdk, *tab_c]
        pre_specs = [tok(d)] + [_full(a.shape) for a in pre_in[1:10]] + [tab()] * 3 \
            + [_full(a.shape) for a in pre_in[13:17]] + [tab()] * 3
        widths = (MLA_HEADS * LANES, MLA_HEADS * LANES, MLA_OUT, SB_OUT, SB_OUT, SB_OUT,
                  DIFF_QK, DIFF_QK, DIFF_OUT)
        qa, ka, va, qb, kb, vb, qc, kc, vc = pl.pallas_call(
            _pre_kernel, grid=(b, n_tm), in_specs=pre_specs,
            out_specs=[tok(wd) for wd in widths], out_shape=[act(wd) for wd in widths],
            compiler_params=cparams2, name="pre_tokens")(*pre_in)

        out_a = pl.pallas_call(
            functools.partial(_softmax2_kernel, mode='mla', p_len=p_len, tile=tqs),
            grid=(b, MLA_HEADS // 2),
            in_specs=[seq_block(2 * LANES), seq_block(2 * LANES), seq_block(LANES)],
            out_specs=seq_block(LANES), scratch_shapes=soft_scratch,
            out_shape=act(MLA_OUT), compiler_params=cparams2, name="attn_mla")(qa, ka, va)

        n_qb = p_len // tqb
        out_b = pl.pallas_call(
            functools.partial(_stick_kernel, tq=tqb, tk=tqb),
            grid=(b, n_qb),
            in_specs=[pl.BlockSpec((None, tqb, SB_OUT), lambda bi, qi: (bi, qi, 0)),
                      pl.BlockSpec((None, p_len, SB_OUT), lambda bi, qi: (bi, 0, 0)),
                      pl.BlockSpec((None, p_len, SB_OUT), lambda bi, qi: (bi, 0, 0)),
                      _full(tri2.shape)],
            out_specs=pl.BlockSpec((None, tqb, SB_OUT), lambda bi, qi: (bi, qi, 0)),
            scratch_shapes=[pltpu.VMEM((SB_HEADS * tqb, LANES), jnp.float32)] * 2,
            out_shape=act(SB_OUT), compiler_params=cparams2s, name="attn_stick")(qb, kb, vb, tri2)

        lam_init = 0.8 - 0.6 * math.exp(-0.3 * layer)
        g_c = _row(diff_out_norm_g[layer])
        out_c = pl.pallas_call(
            functools.partial(_softmax2_kernel, mode='diff', p_len=p_len, tile=tqs, lam_init=lam_init),
            grid=(b, DIFF_HEADS),
            in_specs=[seq_block(LANES), seq_block(LANES), seq_block(LANES),
                      _full((4, DIFF_HEAD_DIM)), _full((1, LANES))],
            out_specs=seq_block(LANES), scratch_shapes=soft_scratch,
            out_shape=act(DIFF_OUT), compiler_params=cparams2, name="attn_diff")(
                qc, kc, vc, diff_lambda[layer].astype(jnp.float32), g_c)

        mg_in = [h_res, g1, wts['w_gate'], _row(gate_b[layer]), out_a, out_b, out_c, wts['wb'], wts['wo']]
        mg_specs = [tok(d), _full(g1.shape), _full(wts['w_gate'].shape), _full((1, N_BRANCHES * D_MODEL)),
                    tok(MLA_OUT), tok(SB_OUT), tok(DIFF_OUT), _full(wts['wb'].shape), _full(wts['wo'].shape)]
        h_res = pl.pallas_call(
            _merge_kernel, grid=(b, n_tm), in_specs=mg_specs, out_specs=tok(d),
            out_shape=jax.ShapeDtypeStruct((b, p_len, d), jnp.float32),
            compiler_params=cparams2, name="merge_out")(*mg_in)

        g2 = _row(ln2_g[layer])
        h_res = pl.pallas_call(
            functools.partial(_ffn_kernel, n_chunks=4), grid=(b, n_tm),
            in_specs=[tok(d), _full(g2.shape), _full(wts['w1'].shape), _full(wts['w2'].shape)],
            out_specs=tok(d), out_shape=jax.ShapeDtypeStruct((b, p_len, d), jnp.float32),
            compiler_params=cparams2, name="ffn")(h_res, g2, wts['w1'], wts['w2'])

    return h_res[:, PAD + N_META:]
```

```python
import functools
import math

import jax
import jax.numpy as jnp
from jax import lax
from jax.experimental import pallas as pl
from jax.experimental.pallas import tpu as pltpu

D_MODEL = 1024
DEPTH = 2
N_META = 16
BLOCK = 128
PAD = BLOCK - N_META
ROPE_THETA = 10000.0
NORM_EPS = 1e-6
MASK_VALUE = -1e30

MLA_HEADS = 8
MLA_Q_LORA = 256
MLA_KV_LORA = 128
MLA_NOPE = 64
MLA_ROPE = 32
MLA_V = 64
MLA_QK = MLA_NOPE + MLA_ROPE
SB_HEADS = 8
SB_HEAD_DIM = 64
DIFF_HEADS = 4
DIFF_HEAD_DIM = 64
DIFF_V_DIM = 2 * DIFF_HEAD_DIM
N_BRANCHES = 3
MLA_OUT = MLA_HEADS * MLA_V
SB_OUT = SB_HEADS * SB_HEAD_DIM
DIFF_QK = DIFF_HEADS * 2 * DIFF_HEAD_DIM
DIFF_OUT = DIFF_HEADS * DIFF_V_DIM
D_FF = 4 * D_MODEL
IN_SIZES = (MLA_Q_LORA, MLA_KV_LORA, MLA_ROPE, SB_OUT, SB_OUT, SB_OUT,
            DIFF_QK, DIFF_QK, DIFF_OUT, N_BRANCHES * D_MODEL)

LANES = 128
HALF_LANES = LANES // 2
MXU_DIM = 256
CHAIN = 128

LOG2E = math.log2(math.e)
VMEM_LIMIT_BYTES = 56 * 1024 * 1024
STICK_SKIP_LOG2 = 110.0 * LOG2E

_NT = (((1,), (1,)), ((), ()))


def _tiles(p_len):
    assert p_len % 528 == 0 and (p_len - BLOCK) % (2 * MXU_DIM) == 0, p_len
    return dict(tm=528, t_soft=2 * MXU_DIM, t_sb=BLOCK)


def _rms(x, g, n):
    ss = jnp.sum(x * x, axis=-1, keepdims=True) * (1.0 / n)
    return (x * lax.rsqrt(ss + NORM_EPS)) * g


def _rope_lanes(x, cos, sina, sinb, half):
    return x * cos + pltpu.roll(x, LANES - half, 1) * sina + pltpu.roll(x, half, 1) * sinb


def _bdot(a, b):
    return jnp.dot(a, b, preferred_element_type=jnp.float32)


def _dot_nt(a, b):
    return lax.dot_general(a, b, _NT, preferred_element_type=jnp.float32)


def _pre_kernel(x_ref, g1_ref, wlat_ref, gcq_ref, gckv_ref, wuq_ref, wk_ref, wv_ref,
                gq_ref, gk_ref, cosa_ref, sinaa_ref, sinba_ref,
                wsb_ref, wdf_ref, gdq_ref, gdk_ref, cosc_ref, sinac_ref, sinbc_ref,
                qa_ref, ka_ref, va_ref, qb_ref, kb_ref, vb_ref, qc_ref, kc_ref, vc_ref):
    bf = jnp.bfloat16
    x = x_ref[...]
    h = _rms(x, g1_ref[...], D_MODEL).astype(bf)

    lat = _bdot(h, wlat_ref[...])
    cq = _rms(lat[:, :MLA_Q_LORA], gcq_ref[...], MLA_Q_LORA).astype(bf)
    ckv = _rms(lat[:, MLA_Q_LORA:MLA_Q_LORA + MLA_KV_LORA], gckv_ref[...], MLA_KV_LORA).astype(bf)
    kr_block = lat[:, MLA_Q_LORA + MLA_KV_LORA:]
    qf = _bdot(cq, wuq_ref[...])
    kf = _bdot(ckv, wk_ref[...])
    va_ref[...] = _bdot(ckv, wv_ref[...]).astype(bf)
    cosa, sinaa, sinba = cosa_ref[...], sinaa_ref[...], sinba_ref[...]
    gq, gk = gq_ref[...], gk_ref[...]
    q_scale = (MLA_QK ** -0.5) * LOG2E
    for hh in range(MLA_HEADS):
        sl = slice(hh * LANES, (hh + 1) * LANES)
        qh = _rope_lanes(_rms(qf[:, sl], gq, MLA_QK), cosa, sinaa, sinba, MLA_ROPE // 2)
        qa_ref[:, sl] = (qh * q_scale).astype(bf)
        kh = _rope_lanes(_rms(kf[:, sl] + kr_block, gk, MLA_QK), cosa, sinaa, sinba, MLA_ROPE // 2)
        ka_ref[:, sl] = kh.astype(bf)

    sb = _bdot(h, wsb_ref[...])
    qb_ref[...] = (sb[:, :SB_OUT] * ((SB_HEAD_DIM ** -0.5) * LOG2E)).astype(bf)
    kb_ref[...] = sb[:, SB_OUT:2 * SB_OUT].astype(bf)
    vb_ref[...] = sb[:, 2 * SB_OUT:].astype(bf)

    df = _bdot(h, wdf_ref[...])
    cosc, sinac, sinbc = cosc_ref[...], sinac_ref[...], sinbc_ref[...]
    lane = lax.broadcasted_iota(jnp.int32, (1, LANES), 1)
    lo = lane < HALF_LANES
    c_scale = (DIFF_HEAD_DIM ** -0.5) * LOG2E

    def half_norm(t, g):
        sq = t * t
        ss_lo = jnp.sum(jnp.where(lo, sq, 0.0), axis=-1, keepdims=True)
        ss_hi = jnp.sum(jnp.where(lo, 0.0, sq), axis=-1, keepdims=True)
        r = jnp.where(lo, lax.rsqrt(ss_lo * (1.0 / DIFF_HEAD_DIM) + NORM_EPS),
                      lax.rsqrt(ss_hi * (1.0 / DIFF_HEAD_DIM) + NORM_EPS))
        return (t * r) * g

    for hh in range(DIFF_HEADS):
        sl = slice(hh * LANES, (hh + 1) * LANES)
        qh = _rope_lanes(half_norm(df[:, sl], gdq_ref[...]), cosc, sinac, sinbc, DIFF_HEAD_DIM // 2)
        qc_ref[:, sl] = (qh * c_scale).astype(bf)
        sk = slice(DIFF_QK + hh * LANES, DIFF_QK + (hh + 1) * LANES)
        kh = _rope_lanes(half_norm(df[:, sk], gdk_ref[...]), cosc, sinac, sinbc, DIFF_HEAD_DIM // 2)
        kc_ref[:, sl] = kh.astype(bf)
    vc_ref[...] = df[:, 2 * DIFF_QK:].astype(bf)


def _softmax2_kernel(*refs, mode, p_len, tile, lam_init=None):
    if mode == 'mla':
        q_ref, k_ref, v_ref, o_ref, m_scr, l_scr, acc_scr, s0_scr, s1_scr, mx0_scr, mx1_scr, vx_scr = refs
    else:
        q_ref, k_ref, v_ref, lam_ref, g_ref, o_ref, m_scr, l_scr, acc_scr, s0_scr, s1_scr, mx0_scr, mx1_scr, vx_scr = refs
    bf = jnp.bfloat16
    kb = MXU_DIM
    assert tile == 2 * kb, (tile, kb)
    n_tiles = (p_len - BLOCK) // tile
    n_kb = (p_len - BLOCK) // kb
    n_half = tile // CHAIN
    n_chain = 2 * n_half
    lane = lax.broadcasted_iota(jnp.int32, (1, LANES), 1)
    lo = lane < HALF_LANES

    if mode == 'diff':
        lp = lam_ref[...]
        lam = (jnp.exp(jnp.sum(lp[0:1] * lp[1:2], axis=-1, keepdims=True))
               - jnp.exp(jnp.sum(lp[2:3] * lp[3:4], axis=-1, keepdims=True)) + lam_init)

    def load_q(r0, n):
        rows = pl.ds(r0, n)
        if mode == 'mla':
            return q_ref[rows, :LANES], q_ref[rows, LANES:]
        q = q_ref[rows, :]
        zero = jnp.zeros_like(q)
        return jnp.where(lo, q, zero), jnp.where(lo, zero, q)

    def k_block(r0, n, m):
        rows = pl.ds(r0, n)
        return k_ref[rows, m * LANES:(m + 1) * LANES] if mode == 'mla' else k_ref[rows, :]

    def combine(o0, o1):
        if mode == 'mla':
            return jnp.where(lo, o0, o1)
        return _rms(o0 - lam * o1, g_ref[...], DIFF_V_DIM) * (1.0 - lam_init)

    row = lax.broadcasted_iota(jnp.int32, (CHAIN, 1), 0)
    col0 = lax.broadcasted_iota(jnp.int32, (1, BLOCK), 1)
    meta_ok = col0 >= PAD

    q_meta = load_q(0, BLOCK)
    v_meta = v_ref[pl.ds(0, BLOCK), :]
    o_meta = []
    for m in range(2):
        s = jnp.where((col0 <= row) & meta_ok, _dot_nt(q_meta[m], k_block(0, BLOCK, m)), MASK_VALUE)
        p = jnp.exp2(s - jnp.max(s, axis=-1, keepdims=True))
        o_meta.append(_bdot(p.astype(bf), v_meta) / jnp.sum(p, axis=-1, keepdims=True))
    o_ref[pl.ds(0, BLOCK), :] = combine(*o_meta).astype(bf)

    col = lax.broadcasted_iota(jnp.int32, (1, kb), 1)
    diag = [[col + d * kb <= row + (c % n_half) * CHAIN for c in range(n_chain)] for d in range(2)]

    def key_row(j):
        return pl.multiple_of(BLOCK + j * kb, BLOCK)

    def qk_into(bufs, qs, r0, masks):
        s_scr, mx_scr = bufs
        for m in range(2):
            s = _dot_nt(qs[m], k_block(r0, kb, m))
            for h in range(n_half):
                c = m * n_half + h
                rs = slice(c * CHAIN, (c + 1) * CHAIN)
                s_c = s[h * CHAIN:(h + 1) * CHAIN]
                if masks is not None:
                    s_c = jnp.where(masks[c], s_c, MASK_VALUE)
                s_scr[rs, :] = s_c
                mx_scr[rs, :] = jnp.broadcast_to(jnp.max(s_c, axis=-1, keepdims=True), (CHAIN, LANES))

    def v_ext(vblk, m):
        zero = jnp.zeros_like(vblk)
        if mode == 'diff':
            return jnp.concatenate([vblk, zero] if m == 0 else [zero, vblk], axis=1)
        mine = lo if m == 0 else jnp.logical_not(lo)
        ones = jnp.broadcast_to(jnp.where(mine, 1.0, 0.0).astype(bf), vblk.shape)
        return jnp.concatenate([jnp.where(mine, vblk, zero), ones], axis=1)

    vx_meta_base = n_kb * 2 * kb
    for m in range(2):
        vx_scr[pl.ds(vx_meta_base + m * BLOCK, BLOCK), :] = v_ext(v_meta, m)

    def build_vx(j, c_):
        vblk = v_ref[pl.ds(key_row(j), kb), :]
        base = pl.multiple_of(j * 2 * kb, 2 * kb)
        vx_scr[pl.ds(base, kb), :] = v_ext(vblk, 0)
        vx_scr[pl.ds(base + kb, kb), :] = v_ext(vblk, 1)
        return c_

    lax.fori_loop(0, n_kb, build_vx, 0)

    def update(get_s, get_max, vx_base, n):
        rhs = vx_scr[pl.ds(vx_base, 2 * n), :]
        for h in range(n_half):
            parts, alphas = [], []
            for m in range(2):
                c = m * n_half + h
                rs = slice(c * CHAIN, (c + 1) * CHAIN)
                s = get_s(c)
                m_run = m_scr[rs, :]
                m_new = jnp.maximum(m_run, get_max(c))
                alpha = jnp.exp2(m_run - m_new)
                m_scr[rs, :] = m_new
                ps = [jnp.exp2(s[:, k * LANES:(k + 1) * LANES] - m_new) for k in range(n // LANES)]
                if mode == 'diff':
                    p_sum = ps[0]
                    for pk in ps[1:]:
                        p_sum = p_sum + pk
                    l_scr[rs, :] = alpha * l_scr[rs, :] + jnp.sum(p_sum, axis=-1, keepdims=True)
                parts += [pk.astype(bf) for pk in ps]
                alphas.append(alpha)
            pv = _bdot(jnp.concatenate(parts, axis=1), rhs)
            if mode == 'mla':
                both = jnp.where(lo, alphas[0], alphas[1])
                alpha_cat = jnp.concatenate([both, both], axis=1)
            else:
                alpha_cat = jnp.concatenate(alphas, axis=1)
            ah = slice(h * CHAIN, (h + 1) * CHAIN)
            acc_scr[ah, :] = alpha_cat * acc_scr[ah, :] + pv

    def update_from(bufs, vx_base):
        s_scr, mx_scr = bufs
        update(lambda c: s_scr[c * CHAIN:(c + 1) * CHAIN, :], lambda c: mx_scr[c * CHAIN:(c + 1) * CHAIN, :],
               vx_base, kb)

    def vx_row(j):
        return pl.multiple_of(j * 2 * kb, 2 * kb)

    buf0, buf1 = (s0_scr, mx0_scr), (s1_scr, mx1_scr)

    def q_tile(i, carry):
        r0 = pl.multiple_of(BLOCK + i * tile, BLOCK)
        qs = load_q(r0, tile)
        m_scr[...] = jnp.full(m_scr.shape, MASK_VALUE, jnp.float32)
        if mode == 'diff':
            l_scr[...] = jnp.zeros(l_scr.shape, jnp.float32)
        acc_scr[...] = jnp.zeros(acc_scr.shape, jnp.float32)
        first = i == 0

        @pl.when(first)
        def _():
            qk_into(buf0, qs, key_row(0), diag[0])

        @pl.when(jnp.logical_not(first))
        def _():
            qk_into(buf0, qs, key_row(0), None)

        s_meta = [jnp.where(meta_ok, _dot_nt(qs[m], k_block(0, BLOCK, m)), MASK_VALUE) for m in range(2)]

        def meta_s(c):
            return s_meta[c // n_half][(c % n_half) * CHAIN:(c % n_half + 1) * CHAIN]

        update(meta_s, lambda c: jnp.max(meta_s(c), axis=-1, keepdims=True), vx_meta_base, BLOCK)

        def pair(t, c_):
            j = 2 * t
            qk_into(buf1, qs, key_row(j + 1), None)
            update_from(buf0, vx_row(j))
            qk_into(buf0, qs, key_row(j + 2), None)
            update_from(buf1, vx_row(j + 1))
            return c_

        lax.fori_loop(0, jnp.maximum(i - 1, 0), pair, 0)
        jd = 2 * i

        @pl.when(jnp.logical_not(first))
        def _():
            qk_into(buf1, qs, key_row(jd - 1), None)
            update_from(buf0, vx_row(jd - 2))
            qk_into(buf0, qs, key_row(jd), diag[0])
            update_from(buf1, vx_row(jd - 1))

        qk_into(buf1, qs, key_row(jd + 1), diag[1])
        update_from(buf0, vx_row(jd))
        update_from(buf1, vx_row(jd + 1))

        for h in range(n_half):
            a = acc_scr[h * CHAIN:(h + 1) * CHAIN, :]
            if mode == 'mla':
                res = a[:, :LANES] / a[:, LANES:]
            else:
                l0 = l_scr[h * CHAIN:(h + 1) * CHAIN, :]
                l1 = l_scr[(n_half + h) * CHAIN:(n_half + h + 1) * CHAIN, :]
                res = combine(a[:, :LANES] / l0, a[:, LANES:] / l1)
            o_ref[pl.ds(r0 + h * CHAIN, CHAIN), :] = res.astype(bf)
        return carry

    lax.fori_loop(0, n_tiles, q_tile, 0)


def _stick_kernel(q_ref, k_ref, v_ref, tri_ref, o_ref, carry_scr, acc_scr, *, tq, tk):
    bf = jnp.bfloat16
    i = pl.program_id(1)
    n_blocks = q_ref.shape[-1] // LANES
    lane = lax.broadcasted_iota(jnp.int32, (1, LANES), 1)
    lo = lane < HALF_LANES
    qs = []
    for blk in range(n_blocks):
        q = q_ref[:, blk * LANES:(blk + 1) * LANES]
        zero = jnp.zeros_like(q)
        qs.append(jnp.concatenate([jnp.where(lo, q, zero), jnp.where(lo, zero, q)], axis=0))
    tri2 = tri_ref[...]
    qpos = i * tq + (lax.broadcasted_iota(jnp.int32, (2 * tq, 1), 0) & (tq - 1))
    carry_scr[...] = jnp.zeros(carry_scr.shape, jnp.float32)
    acc_scr[...] = jnp.zeros(acc_scr.shape, jnp.float32)

    def block_step(j, masked):
        rows = pl.ds(pl.multiple_of(j * tk, tk), tk)
        if masked:
            kpos = j * tk + lax.broadcasted_iota(jnp.int32, (1, tk), 1)
            mask = (kpos < qpos) & (kpos >= PAD)
        zs = [_dot_nt(qs[blk], k_ref[rows, blk * LANES:(blk + 1) * LANES]) for blk in range(n_blocks)]
        sums, worst = [], None
        for blk in range(n_blocks):
            z = zs[blk]
            keep = -(jnp.maximum(z, 0.0) + jnp.log2(1.0 + jnp.exp2(-jnp.abs(z))))
            if masked:
                keep = jnp.where(mask, keep, 0.0)
            hi = keep.astype(bf)
            lw = (keep - hi.astype(jnp.float32)).astype(bf)
            rs = slice(2 * blk * tq, 2 * (blk + 1) * tq)
            carry = carry_scr[rs, :]
            sums.append(carry + _bdot(jnp.concatenate([hi, lw], axis=1), tri2))
            carry = carry + jnp.sum(keep, axis=-1, keepdims=True)
            carry_scr[rs, :] = carry
            worst = carry if worst is None else jnp.maximum(worst, carry)
        for blk in range(n_blocks):
            a = jnp.exp2(zs[blk] + sums[blk])
            if masked:
                a = jnp.where(mask, a, 0.0)
            rs = slice(2 * blk * tq, 2 * (blk + 1) * tq)
            acc_scr[rs, :] += _bdot(a.astype(bf), v_ref[rows, blk * LANES:(blk + 1) * LANES])
        return jnp.max(worst)

    def cond(st):
        j, worst = st
        return (j >= 0) & (worst > -STICK_SKIP_LOG2)

    def body(st):
        j, _ = st
        worst = lax.cond((j == i) | (j == 0), lambda: block_step(j, True), lambda: block_step(j, False))
        return j - 1, worst

    lax.while_loop(cond, body, (i, jnp.float32(0.0)))
    for blk in range(n_blocks):
        r0 = 2 * blk * tq
        o_ref[:, blk * LANES:(blk + 1) * LANES] = jnp.where(
            lo, acc_scr[r0:r0 + tq, :], acc_scr[r0 + tq:r0 + 2 * tq, :]).astype(bf)


def _merge_kernel(x_ref, g1_ref, wg_ref, gb_ref, oa_ref, ob_ref, oc_ref, wb_ref, wo_ref, y_ref):
    bf = jnp.bfloat16
    x = x_ref[...]
    h = _rms(x, g1_ref[...], D_MODEL).astype(bf)
    outs = (oa_ref, ob_ref, oc_ref)
    merged = None
    for br in range(N_BRANCHES):
        cs = slice(br * D_MODEL, (br + 1) * D_MODEL)
        logits = _bdot(h, wg_ref[:, cs]) + gb_ref[:, cs]
        gate = 1.0 / (1.0 + jnp.exp(-logits))
        rs = slice(br * MLA_OUT, (br + 1) * MLA_OUT)
        term = gate * _bdot(outs[br][...], wb_ref[rs, :])
        merged = term if merged is None else merged + term
    y_ref[...] = x + _bdot(merged.astype(bf), wo_ref[...])


def _ffn_kernel(x_ref, g2_ref, w1_ref, w2_ref, y_ref, *, n_chunks):
    bf = jnp.bfloat16
    x = x_ref[...]
    h = _rms(x, g2_ref[...], D_MODEL).astype(bf)
    ck = D_FF // n_chunks
    acc = x
    for c in range(n_chunks):
        cs = slice(c * ck, (c + 1) * ck)
        f = jnp.maximum(_bdot(h, w1_ref[:, cs]), 0.0)
        acc = acc + _bdot((f * f).astype(bf), w2_ref[cs, :])
    y_ref[...] = acc


def _rope_tables(p_len, d, blocks):
    half = d // 2
    pos = jnp.maximum(jnp.arange(p_len) - PAD, 0)
    inv_freq = jnp.exp(-math.log(ROPE_THETA) * (2.0 * jnp.arange(half, dtype=jnp.float32) / d))
    ang = pos.astype(jnp.float32)[:, None] * inv_freq[None, :]
    cos, sin = jnp.cos(ang), jnp.sin(ang)
    zero = jnp.zeros_like(sin)
    cos_p, sina_p, sinb_p, at = [], [], [], 0
    for off in blocks:
        gap = off - at
        cos_p += [jnp.ones((p_len, gap), jnp.float32), cos, cos]
        sina_p += [jnp.zeros((p_len, gap), jnp.float32), -sin, zero]
        sinb_p += [jnp.zeros((p_len, gap), jnp.float32), zero, sin]
        at = off + d
    cos_p.append(jnp.ones((p_len, LANES - at), jnp.float32))
    sina_p.append(jnp.zeros((p_len, LANES - at), jnp.float32))
    sinb_p.append(jnp.zeros((p_len, LANES - at), jnp.float32))
    return tuple(jnp.concatenate(parts, axis=1) for parts in (cos_p, sina_p, sinb_p))


def _row(v):
    return v.reshape(1, -1).astype(jnp.float32)


def _full(shape):
    return pl.BlockSpec(shape, lambda *_: (0,) * len(shape))


def _layer_weights(layer, w_in, mla_w_uq, mla_w_ukv, w_branch, w_out, w_ff1, w_ff2):
    bf = jnp.bfloat16
    w = w_in[layer]
    offs = [0]
    for s in IN_SIZES:
        offs.append(offs[-1] + s)
    d = w.shape[0]
    w_lat = jnp.concatenate([
        w[:, offs[0]:offs[2]],
        jnp.zeros((d, MLA_NOPE), w.dtype), w[:, offs[2]:offs[3]],
        jnp.zeros((d, LANES - MLA_QK), w.dtype)], axis=1).astype(bf)
    w_sb = w[:, offs[3]:offs[6]].astype(bf)
    w_df = w[:, offs[6]:offs[9]].astype(bf)
    w_gate = w[:, offs[9]:offs[10]].astype(bf)
    wuq = mla_w_uq[layer].reshape(MLA_Q_LORA, MLA_HEADS, MLA_QK)
    wuq = jnp.pad(wuq, ((0, 0), (0, 0), (0, LANES - MLA_QK))).reshape(MLA_Q_LORA, MLA_HEADS * LANES).astype(bf)
    wukv = mla_w_ukv[layer].reshape(MLA_KV_LORA, MLA_HEADS, MLA_NOPE + MLA_V)
    wk = jnp.pad(wukv[:, :, :MLA_NOPE], ((0, 0), (0, 0), (0, LANES - MLA_NOPE)))
    wk = wk.reshape(MLA_KV_LORA, MLA_HEADS * LANES).astype(bf)
    wv = wukv[:, :, MLA_NOPE:].reshape(MLA_KV_LORA, MLA_OUT).astype(bf)
    return dict(w_lat=w_lat, w_sb=w_sb, w_df=w_df, w_gate=w_gate, wuq=wuq, wk=wk, wv=wv,
                wb=w_branch[layer].astype(bf), wo=w_out[layer].astype(bf),
                w1=w_ff1[layer].astype(bf), w2=w_ff2[layer].astype(bf))


def kernel(x, meta_tokens, ln1_g, w_in, mla_cq_norm_g, mla_ckv_norm_g, mla_w_uq, mla_w_ukv,
           mla_q_norm_g, mla_k_norm_g, diff_q_norm_g, diff_k_norm_g, diff_lambda,
           diff_out_norm_g, gate_b, w_branch, w_out, ln2_g, w_ff1, w_ff2):
    b, seq, d = x.shape
    assert d == D_MODEL
    bf = jnp.bfloat16
    pad = jnp.zeros((b, PAD, d), x.dtype)
    meta = jnp.broadcast_to(meta_tokens.astype(x.dtype)[None], (b, N_META, d))
    h_res = jnp.concatenate([pad, meta, x], axis=1)
    p_len = h_res.shape[1]
    t = _tiles(p_len)
    tm, tqs, tqb = t['tm'], t['t_soft'], t['t_sb']
    n_tm = p_len // tm

    tab_a = _rope_tables(p_len, MLA_ROPE, [MLA_NOPE])
    tab_c = _rope_tables(p_len, DIFF_HEAD_DIM, [0, HALF_LANES])
    tri = (jnp.arange(tqb)[:, None] >= jnp.arange(tqb)[None, :]).astype(bf)
    tri2 = jnp.concatenate([tri, tri], axis=0)

    cparams2 = pltpu.CompilerParams(dimension_semantics=("parallel", "parallel"),
                                    vmem_limit_bytes=VMEM_LIMIT_BYTES)
    cparams2s = pltpu.CompilerParams(dimension_semantics=("parallel", "arbitrary"),
                                     vmem_limit_bytes=VMEM_LIMIT_BYTES)

    def tok(width):
        return pl.BlockSpec((None, tm, width), lambda bi, ti: (bi, ti, 0))

    def tab():
        return pl.BlockSpec((tm, LANES), lambda bi, ti: (ti, 0))

    def act(width):
        return jax.ShapeDtypeStruct((b, p_len, width), bf)

    soft_scratch = [pltpu.VMEM((2 * tqs, LANES), jnp.float32), pltpu.VMEM((2 * tqs, LANES), jnp.float32),
                    pltpu.VMEM((tqs, 2 * LANES), jnp.float32),
                    pltpu.VMEM((2 * tqs, MXU_DIM), jnp.float32), pltpu.VMEM((2 * tqs, MXU_DIM), jnp.float32),
                    pltpu.VMEM((2 * tqs, LANES), jnp.float32), pltpu.VMEM((2 * tqs, LANES), jnp.float32),
                    pltpu.VMEM((2 * (p_len - BLOCK) + 2 * BLOCK, 2 * LANES), bf)]

    def seq_block(width):
        return pl.BlockSpec((None, p_len, width), lambda bi, ci: (bi, 0, ci))

    for layer in range(DEPTH):
        wts = _layer_weights(layer, w_in, mla_w_uq, mla_w_ukv, w_branch, w_out, w_ff1, w_ff2)
        g1 = _row(ln1_g[layer])
        gq = _row(jnp.pad(mla_q_norm_g[layer], (0, LANES - MLA_QK)))
        gk = _row(jnp.pad(mla_k_norm_g[layer], (0, LANES - MLA_QK)))
        gdq = _row(jnp.tile(diff_q_norm_g[layer], 2))
        gdk = _row(jnp.tile(diff_k_norm_g[layer], 2))

        pre_in = [h_res, g1, wts['w_lat'], _row(mla_cq_norm_g[layer]), _row(mla_ckv_norm_g[layer]),
                  wts['wuq'], wts['wk'], wts['wv'], gq, gk, *tab_a,
                  wts['w_sb'], wts['w_df'], gdq, gdk, *tab_c]
        pre_specs = [tok(d)] + [_full(a.shape) for a in pre_in[1:10]] + [tab()] * 3 \
            + [_full(a.shape) for a in pre_in[13:17]] + [tab()] * 3
        widths = (MLA_HEADS * LANES, MLA_HEADS * LANES, MLA_OUT, SB_OUT, SB_OUT, SB_OUT,
                  DIFF_QK, DIFF_QK, DIFF_OUT)
        qa, ka, va, qb, kb, vb, qc, kc, vc = pl.pallas_call(
            _pre_kernel, grid=(b, n_tm), in_specs=pre_specs,
            out_specs=[tok(wd) for wd in widths], out_shape=[act(wd) for wd in widths],
            compiler_params=cparams2, name="pre_tokens")(*pre_in)

        out_a = pl.pallas_call(
            functools.partial(_softmax2_kernel, mode='mla', p_len=p_len, tile=tqs),
            grid=(b, MLA_HEADS // 2),
            in_specs=[seq_block(2 * LANES), seq_block(2 * LANES), seq_block(LANES)],
            out_specs=seq_block(LANES), scratch_shapes=soft_scratch,
            out_shape=act(MLA_OUT), compiler_params=cparams2, name="attn_mla")(qa, ka, va)

        n_qb = p_len // tqb
        out_b = pl.pallas_call(
            functools.partial(_stick_kernel, tq=tqb, tk=tqb),
            grid=(b, n_qb),
            in_specs=[pl.BlockSpec((None, tqb, SB_OUT), lambda bi, qi: (bi, qi, 0)),
                      pl.BlockSpec((None, p_len, SB_OUT), lambda bi, qi: (bi, 0, 0)),
                      pl.BlockSpec((None, p_len, SB_OUT), lambda bi, qi: (bi, 0, 0)),
                      _full(tri2.shape)],
            out_specs=pl.BlockSpec((None, tqb, SB_OUT), lambda bi, qi: (bi, qi, 0)),
            scratch_shapes=[pltpu.VMEM((SB_HEADS * tqb, LANES), jnp.float32)] * 2,
            out_shape=act(SB_OUT), compiler_params=cparams2s, name="attn_stick")(qb, kb, vb, tri2)

        lam_init = 0.8 - 0.6 * math.exp(-0.3 * layer)
        g_c = _row(diff_out_norm_g[layer])
        out_c = pl.pallas_call(
            functools.partial(_softmax2_kernel, mode='diff', p_len=p_len, tile=tqs, lam_init=lam_init),
            grid=(b, DIFF_HEADS),
            in_specs=[seq_block(LANES), seq_block(LANES), seq_block(LANES),
                      _full((4, DIFF_HEAD_DIM)), _full((1, LANES))],
            out_specs=seq_block(LANES), scratch_shapes=soft_scratch,
            out_shape=act(DIFF_OUT), compiler_params=cparams2, name="attn_diff")(
                qc, kc, vc, diff_lambda[layer].astype(jnp.float32), g_c)

        mg_in = [h_res, g1, wts['w_gate'], _row(gate_b[layer]), out_a, out_b, out_c, wts['wb'], wts['wo']]
        mg_specs = [tok(d), _full(g1.shape), _full(wts['w_gate'].shape), _full((1, N_BRANCHES * D_MODEL)),
                    tok(MLA_OUT), tok(SB_OUT), tok(DIFF_OUT), _full(wts['wb'].shape), _full(wts['wo'].shape)]
        h_res = pl.pallas_call(
            _merge_kernel, grid=(b, n_tm), in_specs=mg_specs, out_specs=tok(d),
            out_shape=jax.ShapeDtypeStruct((b, p_len, d), jnp.float32),
            compiler_params=cparams2, name="merge_out")(*mg_in)

        g2 = _row(ln2_g[layer])
        h_res = pl.pallas_call(
            functools.partial(_ffn_kernel, n_chunks=4), grid=(b, n_tm),
            in_specs=[tok(d), _full(g2.shape), _full(wts['w1'].shape), _full(wts['w2'].shape)],
            out_specs=tok(d), out_shape=jax.ShapeDtypeStruct((b, p_len, d), jnp.float32),
            compiler_params=cparams2, name="ffn")(h_res, g2, wts['w1'], wts['w2'])

    return h_res[:, PAD + N_META:]
```

```python
import functools
import math

import jax
import jax.numpy as jnp
from jax import lax
from jax.experimental import pallas as pl
from jax.experimental.pallas import tpu as pltpu

D_MODEL = 1024
DEPTH = 2
N_META = 16
BLOCK = 128
PAD = BLOCK - N_META
ROPE_THETA = 10000.0
NORM_EPS = 1e-6
MASK_VALUE = -1e30

MLA_HEADS = 8
MLA_Q_LORA = 256
MLA_KV_LORA = 128
MLA_NOPE = 64
MLA_ROPE = 32
MLA_V = 64
MLA_QK = MLA_NOPE + MLA_ROPE
SB_HEADS = 8
SB_HEAD_DIM = 64
DIFF_HEADS = 4
DIFF_HEAD_DIM = 64
DIFF_V_DIM = 2 * DIFF_HEAD_DIM
N_BRANCHES = 3
MLA_OUT = MLA_HEADS * MLA_V
SB_OUT = SB_HEADS * SB_HEAD_DIM
DIFF_QK = DIFF_HEADS * 2 * DIFF_HEAD_DIM
DIFF_OUT = DIFF_HEADS * DIFF_V_DIM
D_FF = 4 * D_MODEL
IN_SIZES = (MLA_Q_LORA, MLA_KV_LORA, MLA_ROPE, SB_OUT, SB_OUT, SB_OUT,
            DIFF_QK, DIFF_QK, DIFF_OUT, N_BRANCHES * D_MODEL)

LANES = 128
HALF_LANES = LANES // 2
MXU_DIM = 256
CHAIN = 128

LOG2E = math.log2(math.e)
VMEM_LIMIT_BYTES = 56 * 1024 * 1024
STICK_SKIP_LOG2 = 110.0 * LOG2E

_NT = (((1,), (1,)), ((), ()))


def _tiles(p_len):
    assert p_len % 528 == 0 and (p_len - BLOCK) % (2 * MXU_DIM) == 0, p_len
    return dict(tm=528, t_soft=2 * MXU_DIM, t_sb=BLOCK)


def _rms(x, g, n):
    ss = jnp.sum(x * x, axis=-1, keepdims=True) * (1.0 / n)
    return (x * lax.rsqrt(ss + NORM_EPS)) * g


def _rope_lanes(x, cos, sina, sinb, half):
    return x * cos + pltpu.roll(x, LANES - half, 1) * sina + pltpu.roll(x, half, 1) * sinb


def _bdot(a, b):
    return jnp.dot(a, b, preferred_element_type=jnp.float32)


def _dot_nt(a, b):
    return lax.dot_general(a, b, _NT, preferred_element_type=jnp.float32)


def _pre_kernel(x_ref, g1_ref, wlat_ref, gcq_ref, gckv_ref, wuq_ref, wuqp_ref, wk_ref, wv_ref,
                c1q_ref, c2q_ref, c1k_ref, c2k_ref,
                wsb_ref, wdf_ref, gdq_ref, gdk_ref, cosc_ref, sinac_ref, sinbc_ref,
                qa_ref, ka_ref, va_ref, qb_ref, kb_ref, vb_ref, qc_ref, kc_ref, vc_ref):
    bf = jnp.bfloat16
    x = x_ref[...]
    h = _rms(x, g1_ref[...], D_MODEL).astype(bf)

    df = _bdot(h, wdf_ref[...])
    cosc, sinac, sinbc = cosc_ref[...], sinac_ref[...], sinbc_ref[...]
    lane = lax.broadcasted_iota(jnp.int32, (1, LANES), 1)
    lo = lane < HALF_LANES
    c_scale = (DIFF_HEAD_DIM ** -0.5) * LOG2E

    def half_norm(t, g):
        sq = t * t
        ss_lo = jnp.sum(jnp.where(lo, sq, 0.0), axis=-1, keepdims=True)
        ss_hi = jnp.sum(jnp.where(lo, 0.0, sq), axis=-1, keepdims=True)
        r = jnp.where(lo, lax.rsqrt(ss_lo * (1.0 / DIFF_HEAD_DIM) + NORM_EPS),
                      lax.rsqrt(ss_hi * (1.0 / DIFF_HEAD_DIM) + NORM_EPS))
        return (t * r) * g

    for hh in range(DIFF_HEADS):
        sl = slice(hh * LANES, (hh + 1) * LANES)
        qh = _rope_lanes(half_norm(df[:, sl], gdq_ref[...]), cosc, sinac, sinbc, DIFF_HEAD_DIM // 2)
        qc_ref[:, sl] = (qh * c_scale).astype(bf)
        sk = slice(DIFF_QK + hh * LANES, DIFF_QK + (hh + 1) * LANES)
        kh = _rope_lanes(half_norm(df[:, sk], gdk_ref[...]), cosc, sinac, sinbc, DIFF_HEAD_DIM // 2)
        kc_ref[:, sl] = kh.astype(bf)
    vc_ref[...] = df[:, 2 * DIFF_QK:].astype(bf)

    lat = _bdot(h, wlat_ref[...])
    cq = _rms(lat[:, :MLA_Q_LORA], gcq_ref[...], MLA_Q_LORA).astype(bf)
    ckv = _rms(lat[:, MLA_Q_LORA:MLA_Q_LORA + MLA_KV_LORA], gckv_ref[...], MLA_KV_LORA).astype(bf)
    kr_off = MLA_Q_LORA + MLA_KV_LORA
    kr_block = lat[:, kr_off:kr_off + LANES]
    krp_c2 = lat[:, kr_off + LANES:] * c2k_ref[...]
    qf = _bdot(cq, wuq_ref[...])
    qpf = _bdot(cq, wuqp_ref[...])
    kf = _bdot(ckv, wk_ref[...])
    va_ref[...] = _bdot(ckv, wv_ref[...]).astype(bf)
    c1q, c2q, c1k = c1q_ref[...], c2q_ref[...], c1k_ref[...]

    def inv_rms(t):
        return lax.rsqrt(jnp.sum(t * t, axis=-1, keepdims=True) * (1.0 / MLA_QK) + NORM_EPS)

    for hh in range(MLA_HEADS):
        sl = slice(hh * LANES, (hh + 1) * LANES)
        qx = qf[:, sl]
        qa_ref[:, sl] = ((qx * c1q + qpf[:, sl] * c2q) * inv_rms(qx)).astype(bf)
        kx = kf[:, sl] + kr_block
        ka_ref[:, sl] = ((kx * c1k + krp_c2) * inv_rms(kx)).astype(bf)

    sb = _bdot(h, wsb_ref[...])
    qb_ref[...] = (sb[:, :SB_OUT] * ((SB_HEAD_DIM ** -0.5) * LOG2E)).astype(bf)
    kb_ref[...] = sb[:, SB_OUT:2 * SB_OUT].astype(bf)
    vb_ref[...] = sb[:, 2 * SB_OUT:].astype(bf)


def _softmax2_kernel(*refs, mode, p_len, tile, lam_init=None):
    if mode == 'mla':
        q_ref, k_ref, v_ref, o_ref, m_scr, acc_scr, s0_scr, s1_scr, mx0_scr, mx1_scr, vx_scr = refs
    else:
        q_ref, k_ref, v_ref, lam_ref, g_ref, o_ref, m_scr, acc_scr, s0_scr, s1_scr, mx0_scr, mx1_scr, vx_scr = refs
    bf = jnp.bfloat16
    kb = MXU_DIM
    assert tile == 2 * kb, (tile, kb)
    n_tiles = (p_len - BLOCK) // tile
    n_kb = (p_len - BLOCK) // kb
    n_half = tile // CHAIN
    n_chain = 2 * n_half
    lane = lax.broadcasted_iota(jnp.int32, (1, LANES), 1)
    lo = lane < HALF_LANES

    if mode == 'diff':
        lp = lam_ref[...]
        lam = (jnp.exp(jnp.sum(lp[0:1] * lp[1:2], axis=-1, keepdims=True))
               - jnp.exp(jnp.sum(lp[2:3] * lp[3:4], axis=-1, keepdims=True)) + lam_init)

    def load_q(r0, n):
        rows = pl.ds(r0, n)
        if mode == 'mla':
            return q_ref[rows, :LANES], q_ref[rows, LANES:]
        q = q_ref[rows, :]
        return q, q

    def k_block(r0, n, m):
        rows = pl.ds(r0, n)
        if mode == 'mla':
            return k_ref[rows, m * LANES:(m + 1) * LANES]
        k = k_ref[rows, :]
        return jnp.where(lo if m == 0 else jnp.logical_not(lo), k, jnp.zeros_like(k))

    def combine(o0, o1):
        if mode == 'mla':
            return jnp.where(lo, o0, o1)
        return _rms(o0 - lam * o1, g_ref[...], DIFF_V_DIM) * (1.0 - lam_init)

    row = lax.broadcasted_iota(jnp.int32, (CHAIN, 1), 0)
    col0 = lax.broadcasted_iota(jnp.int32, (1, BLOCK), 1)
    meta_ok = col0 >= PAD

    q_meta = load_q(0, BLOCK)
    v_meta = v_ref[pl.ds(0, BLOCK), :]
    o_meta = []
    for m in range(2):
        s = jnp.where((col0 <= row) & meta_ok, _dot_nt(q_meta[m], k_block(0, BLOCK, m)), MASK_VALUE)
        p = jnp.exp2(s - jnp.max(s, axis=-1, keepdims=True))
        o_meta.append(_bdot(p.astype(bf), v_meta) / jnp.sum(p, axis=-1, keepdims=True))
    o_ref[pl.ds(0, BLOCK), :] = combine(*o_meta).astype(bf)

    col = lax.broadcasted_iota(jnp.int32, (1, kb), 1)
    diag = [[col + d * kb <= row + (c % n_half) * CHAIN for c in range(n_chain)] for d in range(2)]

    def key_row(j):
        return pl.multiple_of(BLOCK + j * kb, BLOCK)

    def qk_into(bufs, qs, r0, masks, h_from=0):
        s_scr, mx_scr = bufs
        for m in range(2):
            s = _dot_nt(qs[m][h_from * CHAIN:], k_block(r0, kb, m))
            for h in range(h_from, n_half):
                c = m * n_half + h
                rs = slice(c * CHAIN, (c + 1) * CHAIN)
                s_c = s[(h - h_from) * CHAIN:(h - h_from + 1) * CHAIN]
                if masks is not None:
                    s_c = jnp.where(masks[c], s_c, MASK_VALUE)
                s_scr[rs, :] = s_c
                mx_scr[rs, :] = jnp.broadcast_to(jnp.max(s_c, axis=-1, keepdims=True), (CHAIN, LANES))

    v_maps = 2 if mode == 'mla' else 1

    def v_ext(vblk, m):
        if mode == 'diff':
            return jnp.concatenate([vblk, jnp.ones_like(vblk)], axis=1)
        mine = lo if m == 0 else jnp.logical_not(lo)
        ones = jnp.broadcast_to(jnp.where(mine, 1.0, 0.0).astype(bf), vblk.shape)
        return jnp.concatenate([jnp.where(mine, vblk, jnp.zeros_like(vblk)), ones], axis=1)

    vx_meta_base = n_kb * v_maps * kb
    for m in range(v_maps):
        vx_scr[pl.ds(vx_meta_base + m * BLOCK, BLOCK), :] = v_ext(v_meta, m)

    def build_vx(j, c_):
        vblk = v_ref[pl.ds(key_row(j), kb), :]
        base = pl.multiple_of(j * v_maps * kb, kb)
        for m in range(v_maps):
            vx_scr[pl.ds(base + m * kb, kb), :] = v_ext(vblk, m)
        return c_

    lax.fori_loop(0, n_kb, build_vx, 0)

    def update(get_s, get_max, vx_base, n, h_from=0):
        rhs = vx_scr[pl.ds(vx_base, v_maps * n), :]
        for h in range(h_from, n_half):
            parts, alphas = [], []
            for m in range(2):
                c = m * n_half + h
                rs = slice(c * CHAIN, (c + 1) * CHAIN)
                s = get_s(c)
                m_run = m_scr[rs, :]
                m_new = jnp.maximum(m_run, get_max(c))
                alpha = jnp.exp2(m_run - m_new)
                m_scr[rs, :] = m_new
                ps = [jnp.exp2(s[:, k * LANES:(k + 1) * LANES] - m_new).astype(bf) for k in range(n // LANES)]
                if mode == 'diff':
                    p = ps[0] if len(ps) == 1 else jnp.concatenate(ps, axis=1)
                    acc_scr[rs, :] = jnp.concatenate([alpha, alpha], axis=1) * acc_scr[rs, :] + _bdot(p, rhs)
                parts += ps
                alphas.append(alpha)
            if mode == 'mla':
                both = jnp.where(lo, alphas[0], alphas[1])
                ah = slice(h * CHAIN, (h + 1) * CHAIN)
                acc_scr[ah, :] = (jnp.concatenate([both, both], axis=1) * acc_scr[ah, :]
                                  + _bdot(jnp.concatenate(parts, axis=1), rhs))

    def update_from(bufs, vx_base, h_from=0):
        s_scr, mx_scr = bufs
        update(lambda c: s_scr[c * CHAIN:(c + 1) * CHAIN, :], lambda c: mx_scr[c * CHAIN:(c + 1) * CHAIN, :],
               vx_base, kb, h_from)

    def vx_row(j):
        return pl.multiple_of(j * v_maps * kb, kb)

    buf0, buf1 = (s0_scr, mx0_scr), (s1_scr, mx1_scr)

    def q_tile(i, carry):
        r0 = pl.multiple_of(BLOCK + i * tile, BLOCK)
        qs = load_q(r0, tile)
        m_scr[...] = jnp.full(m_scr.shape, MASK_VALUE, jnp.float32)
        acc_scr[...] = jnp.zeros(acc_scr.shape, jnp.float32)
        first = i == 0
        qk_into(buf0, qs, key_row(0), [col <= row + (c % n_half) * CHAIN + i * tile for c in range(n_chain)])

        s_meta = [jnp.where(meta_ok, _dot_nt(qs[m], k_block(0, BLOCK, m)), MASK_VALUE) for m in range(2)]

        def meta_s(c):
            return s_meta[c // n_half][(c % n_half) * CHAIN:(c % n_half + 1) * CHAIN]

        update(meta_s, lambda c: jnp.max(meta_s(c), axis=-1, keepdims=True), vx_meta_base, BLOCK)

        def pair(t, c_):
            j = 2 * t
            qk_into(buf1, qs, key_row(j + 1), None)
            update_from(buf0, vx_row(j))
            qk_into(buf0, qs, key_row(j + 2), None)
            update_from(buf1, vx_row(j + 1))
            return c_

        lax.fori_loop(0, jnp.maximum(i - 1, 0), pair, 0)
        jd = 2 * i

        @pl.when(jnp.logical_not(first))
        def _():
            qk_into(buf1, qs, key_row(jd - 1), None)
            update_from(buf0, vx_row(jd - 2))
            qk_into(buf0, qs, key_row(jd), diag[0])
            update_from(buf1, vx_row(jd - 1))

        h_diag1 = kb // CHAIN
        qk_into(buf1, qs, key_row(jd + 1), diag[1], h_diag1)
        update_from(buf0, vx_row(jd))
        update_from(buf1, vx_row(jd + 1), h_diag1)

        for h in range(n_half):
            a = acc_scr[h * CHAIN:(h + 1) * CHAIN, :]
            if mode == 'mla':
                res = a[:, :LANES] / a[:, LANES:]
            else:
                a1 = acc_scr[(n_half + h) * CHAIN:(n_half + h + 1) * CHAIN, :]
                res = combine(a[:, :LANES] / a[:, LANES:], a1[:, :LANES] / a1[:, LANES:])
            o_ref[pl.ds(r0 + h * CHAIN, CHAIN), :] = res.astype(bf)
        return carry

    lax.fori_loop(0, n_tiles, q_tile, 0)


def _stick_kernel(q_ref, k_ref, v_ref, tri_ref, o_ref, carry_scr, acc_scr, *, tq, tk):
    bf = jnp.bfloat16
    i = pl.program_id(1)
    n_blocks = q_ref.shape[-1] // LANES
    lane = lax.broadcasted_iota(jnp.int32, (1, LANES), 1)
    lo = lane < HALF_LANES
    qs = []
    for blk in range(n_blocks):
        q = q_ref[:, blk * LANES:(blk + 1) * LANES]
        zero = jnp.zeros_like(q)
        qs.append(jnp.concatenate([jnp.where(lo, q, zero), jnp.where(lo, zero, q)], axis=0))
    tri2 = tri_ref[...]
    qpos = i * tq + (lax.broadcasted_iota(jnp.int32, (2 * tq, 1), 0) & (tq - 1))
    carry_scr[...] = jnp.zeros(carry_scr.shape, jnp.float32)
    acc_scr[...] = jnp.zeros(acc_scr.shape, jnp.float32)

    def block_step(j, masked):
        rows = pl.ds(pl.multiple_of(j * tk, tk), tk)
        if masked:
            kpos = j * tk + lax.broadcasted_iota(jnp.int32, (1, tk), 1)
            mask = (kpos < qpos) & (kpos >= PAD)
        zs = [_dot_nt(qs[blk], k_ref[rows, blk * LANES:(blk + 1) * LANES]) for blk in range(n_blocks)]
        sums, worst = [], None
        for blk in range(n_blocks):
            z = zs[blk]
            keep = -(jnp.maximum(z, 0.0) + jnp.log2(1.0 + jnp.exp2(-jnp.abs(z))))
            if masked:
                keep = jnp.where(mask, keep, 0.0)
            hi = keep.astype(bf)
            lw = (keep - hi.astype(jnp.float32)).astype(bf)
            rs = slice(2 * blk * tq, 2 * (blk + 1) * tq)
            carry = carry_scr[rs, :]
            sums.append(carry + _bdot(jnp.concatenate([hi, lw], axis=1), tri2))
            carry = carry + jnp.sum(keep, axis=-1, keepdims=True)
            carry_scr[rs, :] = carry
            worst = carry if worst is None else jnp.maximum(worst, carry)
        for blk in range(n_blocks):
            a = jnp.exp2(zs[blk] + sums[blk])
            if masked:
                a = jnp.where(mask, a, 0.0)
            rs = slice(2 * blk * tq, 2 * (blk + 1) * tq)
            acc_scr[rs, :] += _bdot(a.astype(bf), v_ref[rows, blk * LANES:(blk + 1) * LANES])
        return jnp.max(worst)

    def cond(st):
        j, worst = st
        return (j >= 0) & (worst > -STICK_SKIP_LOG2)

    def body(st):
        j, _ = st
        worst = lax.cond((j == i) | (j == 0), lambda: block_step(j, True), lambda: block_step(j, False))
        return j - 1, worst

    lax.while_loop(cond, body, (i, jnp.float32(0.0)))
    for blk in range(n_blocks):
        r0 = 2 * blk * tq
        o_ref[:, blk * LANES:(blk + 1) * LANES] = jnp.where(
            lo, acc_scr[r0:r0 + tq, :], acc_scr[r0 + tq:r0 + 2 * tq, :]).astype(bf)


def _merge_kernel(x_ref, g1_ref, wg_ref, gb_ref, oa_ref, ob_ref, oc_ref, wb_ref, wo_ref, y_ref):
    bf = jnp.bfloat16
    x = x_ref[...]
    h = _rms(x, g1_ref[...], D_MODEL).astype(bf)
    outs = (oa_ref, ob_ref, oc_ref)
    merged = None
    for br in range(N_BRANCHES):
        cs = slice(br * D_MODEL, (br + 1) * D_MODEL)
        logits = _bdot(h, wg_ref[:, cs]) + gb_ref[:, cs]
        gate = 1.0 / (1.0 + jnp.exp(-logits))
        rs = slice(br * MLA_OUT, (br + 1) * MLA_OUT)
        term = gate * _bdot(outs[br][...], wb_ref[rs, :])
        merged = term if merged is None else merged + term
    y_ref[...] = x + _bdot(merged.astype(bf), wo_ref[...])


def _ffn_kernel(x_ref, g2_ref, w1_ref, w2_ref, y_ref, *, n_chunks):
    bf = jnp.bfloat16
    x = x_ref[...]
    h = _rms(x, g2_ref[...], D_MODEL).astype(bf)
    ck = D_FF // n_chunks
    acc = x
    for c in range(n_chunks):
        cs = slice(c * ck, (c + 1) * ck)
        f = jnp.maximum(_bdot(h, w1_ref[:, cs]), 0.0)
        acc = acc + _bdot((f * f).astype(bf), w2_ref[cs, :])
    y_ref[...] = acc


def _rope_tables(p_len, d, blocks):
    half = d // 2
    pos = jnp.maximum(jnp.arange(p_len) - PAD, 0)
    inv_freq = jnp.exp(-math.log(ROPE_THETA) * (2.0 * jnp.arange(half, dtype=jnp.float32) / d))
    ang = pos.astype(jnp.float32)[:, None] * inv_freq[None, :]
    cos, sin = jnp.cos(ang), jnp.sin(ang)
    zero = jnp.zeros_like(sin)
    cos_p, sina_p, sinb_p, at = [], [], [], 0
    for off in blocks:
        gap = off - at
        cos_p += [jnp.ones((p_len, gap), jnp.float32), cos, cos]
        sina_p += [jnp.zeros((p_len, gap), jnp.float32), -sin, zero]
        sinb_p += [jnp.zeros((p_len, gap), jnp.float32), zero, sin]
        at = off + d
    cos_p.append(jnp.ones((p_len, LANES - at), jnp.float32))
    sina_p.append(jnp.zeros((p_len, LANES - at), jnp.float32))
    sinb_p.append(jnp.zeros((p_len, LANES - at), jnp.float32))
    return tuple(jnp.concatenate(parts, axis=1) for parts in (cos_p, sina_p, sinb_p))


def _swap_halves(t):
    half = t.shape[-1] // 2
    return jnp.concatenate([t[..., half:], t[..., :half]], axis=-1)


def _row(v):
    return v.reshape(1, -1).astype(jnp.float32)


def _full(shape):
    return pl.BlockSpec(shape, lambda *_: (0,) * len(shape))


def _layer_weights(layer, w_in, mla_w_uq, mla_w_ukv, w_branch, w_out, w_ff1, w_ff2):
    bf = jnp.bfloat16
    w = w_in[layer]
    offs = [0]
    for s in IN_SIZES:
        offs.append(offs[-1] + s)
    d = w.shape[0]
    w_kr = w[:, offs[2]:offs[3]]
    w_lat = jnp.concatenate([
        w[:, offs[0]:offs[2]],
        jnp.zeros((d, MLA_NOPE), w.dtype), w_kr,
        jnp.zeros((d, LANES - MLA_QK), w.dtype),
        jnp.zeros((d, MLA_NOPE), w.dtype), _swap_halves(w_kr),
        jnp.zeros((d, LANES - MLA_QK), w.dtype)], axis=1).astype(bf)
    w_sb = w[:, offs[3]:offs[6]].astype(bf)
    w_df = w[:, offs[6]:offs[9]].astype(bf)
    w_gate = w[:, offs[9]:offs[10]].astype(bf)
    wuq = mla_w_uq[layer].reshape(MLA_Q_LORA, MLA_HEADS, MLA_QK)
    wuqp = jnp.concatenate([wuq[..., :MLA_NOPE], _swap_halves(wuq[..., MLA_NOPE:])], axis=-1)
    wuq = jnp.pad(wuq, ((0, 0), (0, 0), (0, LANES - MLA_QK))).reshape(MLA_Q_LORA, MLA_HEADS * LANES).astype(bf)
    wuqp = jnp.pad(wuqp, ((0, 0), (0, 0), (0, LANES - MLA_QK))).reshape(MLA_Q_LORA, MLA_HEADS * LANES).astype(bf)
    wukv = mla_w_ukv[layer].reshape(MLA_KV_LORA, MLA_HEADS, MLA_NOPE + MLA_V)
    wk = jnp.pad(wukv[:, :, :MLA_NOPE], ((0, 0), (0, 0), (0, LANES - MLA_NOPE)))
    wk = wk.reshape(MLA_KV_LORA, MLA_HEADS * LANES).astype(bf)
    wv = wukv[:, :, MLA_NOPE:].reshape(MLA_KV_LORA, MLA_OUT).astype(bf)
    return dict(w_lat=w_lat, w_sb=w_sb, w_df=w_df, w_gate=w_gate, wuq=wuq, wuqp=wuqp, wk=wk, wv=wv,
                wb=w_branch[layer].astype(bf), wo=w_out[layer].astype(bf),
                w1=w_ff1[layer].astype(bf), w2=w_ff2[layer].astype(bf))


def kernel(x, meta_tokens, ln1_g, w_in, mla_cq_norm_g, mla_ckv_norm_g, mla_w_uq, mla_w_ukv,
           mla_q_norm_g, mla_k_norm_g, diff_q_norm_g, diff_k_norm_g, diff_lambda,
           diff_out_norm_g, gate_b, w_branch, w_out, ln2_g, w_ff1, w_ff2):
    b, seq, d = x.shape
    assert d == D_MODEL
    bf = jnp.bfloat16
    pad = jnp.zeros((b, PAD, d), x.dtype)
    meta = jnp.broadcast_to(meta_tokens.astype(x.dtype)[None], (b, N_META, d))
    h_res = jnp.concatenate([pad, meta, x], axis=1)
    p_len = h_res.shape[1]
    t = _tiles(p_len)
    tm, tqs, tqb = t['tm'], t['t_soft'], t['t_sb']
    n_tm = p_len // tm

    tab_a = _rope_tables(p_len, MLA_ROPE, [MLA_NOPE])
    tab_c = _rope_tables(p_len, DIFF_HEAD_DIM, [0, HALF_LANES])
    tri = (jnp.arange(tqb)[:, None] >= jnp.arange(tqb)[None, :]).astype(bf)
    tri2 = jnp.concatenate([tri, tri], axis=0)

    cparams2 = pltpu.CompilerParams(dimension_semantics=("parallel", "parallel"),
                                    vmem_limit_bytes=VMEM_LIMIT_BYTES)
    cparams2s = pltpu.CompilerParams(dimension_semantics=("parallel", "arbitrary"),
                                     vmem_limit_bytes=VMEM_LIMIT_BYTES)

    def tok(width):
        return pl.BlockSpec((None, tm, width), lambda bi, ti: (bi, ti, 0))

    def tab():
        return pl.BlockSpec((tm, LANES), lambda bi, ti: (ti, 0))

    def act(width):
        return jax.ShapeDtypeStruct((b, p_len, width), bf)

    def soft_scratch(v_maps):
        return [pltpu.VMEM((2 * tqs, LANES), jnp.float32),
                pltpu.VMEM((2 * tqs, 2 * LANES), jnp.float32),
                pltpu.VMEM((2 * tqs, MXU_DIM), jnp.float32), pltpu.VMEM((2 * tqs, MXU_DIM), jnp.float32),
                pltpu.VMEM((2 * tqs, LANES), jnp.float32), pltpu.VMEM((2 * tqs, LANES), jnp.float32),
                pltpu.VMEM((v_maps * p_len, 2 * LANES), bf)]

    def seq_block(width):
        return pl.BlockSpec((None, p_len, width), lambda bi, ci: (bi, 0, ci))

    for layer in range(DEPTH):
        wts = _layer_weights(layer, w_in, mla_w_uq, mla_w_ukv, w_branch, w_out, w_ff1, w_ff2)
        g1 = _row(ln1_g[layer])
        gdq = _row(jnp.tile(diff_q_norm_g[layer], 2))
        gdk = _row(jnp.tile(diff_k_norm_g[layer], 2))

        def mla_tables(gain, scale):
            g = gain.astype(jnp.float32)
            g_own = _row(jnp.pad(g, (0, LANES - MLA_QK)))
            g_partner = _row(jnp.pad(jnp.concatenate([g[:MLA_NOPE], _swap_halves(g[MLA_NOPE:])]),
                                     (0, LANES - MLA_QK)))
            return tab_a[0] * (g_own * scale), (tab_a[1] + tab_a[2]) * (g_partner * scale)

        c1q, c2q = mla_tables(mla_q_norm_g[layer], (MLA_QK ** -0.5) * LOG2E)
        c1k, c2k = mla_tables(mla_k_norm_g[layer], 1.0)

        pre_in = [h_res, g1, wts['w_lat'], _row(mla_cq_norm_g[layer]), _row(mla_ckv_norm_g[layer]),
                  wts['wuq'], wts['wuqp'], wts['wk'], wts['wv'], c1q, c2q, c1k, c2k,
                  wts['w_sb'], wts['w_df'], gdq, gdk, *tab_c]
        pre_specs = [tok(d)] + [_full(a.shape) for a in pre_in[1:9]] + [tab()] * 4 \
            + [_full(a.shape) for a in pre_in[13:17]] + [tab()] * 3
        widths = (MLA_HEADS * LANES, MLA_HEADS * LANES, MLA_OUT, SB_OUT, SB_OUT, SB_OUT,
                  DIFF_QK, DIFF_QK, DIFF_OUT)
        qa, ka, va, qb, kb, vb, qc, kc, vc = pl.pallas_call(
            _pre_kernel, grid=(b, n_tm), in_specs=pre_specs,
            out_specs=[tok(wd) for wd in widths], out_shape=[act(wd) for wd in widths],
            compiler_params=cparams2, name="pre_tokens")(*pre_in)

        out_a = pl.pallas_call(
            functools.partial(_softmax2_kernel, mode='mla', p_len=p_len, tile=tqs),
            grid=(b, MLA_HEADS // 2),
            in_specs=[seq_block(2 * LANES), seq_block(2 * LANES), seq_block(LANES)],
            out_specs=seq_block(LANES), scratch_shapes=soft_scratch(2),
            out_shape=act(MLA_OUT), compiler_params=cparams2, name="attn_mla")(qa, ka, va)

        n_qb = p_len // tqb
        out_b = pl.pallas_call(
            functools.partial(_stick_kernel, tq=tqb, tk=tqb),
            grid=(b, n_qb),
            in_specs=[pl.BlockSpec((None, tqb, SB_OUT), lambda bi, qi: (bi, qi, 0)),
                      pl.BlockSpec((None, p_len, SB_OUT), lambda bi, qi: (bi, 0, 0)),
                      pl.BlockSpec((None, p_len, SB_OUT), lambda bi, qi: (bi, 0, 0)),
                      _full(tri2.shape)],
            out_specs=pl.BlockSpec((None, tqb, SB_OUT), lambda bi, qi: (bi, qi, 0)),
            scratch_shapes=[pltpu.VMEM((SB_HEADS * tqb, LANES), jnp.float32)] * 2,
            out_shape=act(SB_OUT), compiler_params=cparams2s, name="attn_stick")(qb, kb, vb, tri2)

        lam_init = 0.8 - 0.6 * math.exp(-0.3 * layer)
        g_c = _row(diff_out_norm_g[layer])
        out_c = pl.pallas_call(
            functools.partial(_softmax2_kernel, mode='diff', p_len=p_len, tile=tqs, lam_init=lam_init),
            grid=(b, DIFF_HEADS),
            in_specs=[seq_block(LANES), seq_block(LANES), seq_block(LANES),
                      _full((4, DIFF_HEAD_DIM)), _full((1, LANES))],
            out_specs=seq_block(LANES), scratch_shapes=soft_scratch(1),
            out_shape=act(DIFF_OUT), compiler_params=cparams2, name="attn_diff")(
                qc, kc, vc, diff_lambda[layer].astype(jnp.float32), g_c)

        mg_in = [h_res, g1, wts['w_gate'], _row(gate_b[layer]), out_a, out_b, out_c, wts['wb'], wts['wo']]
        mg_specs = [tok(d), _full(g1.shape), _full(wts['w_gate'].shape), _full((1, N_BRANCHES * D_MODEL)),
                    tok(MLA_OUT), tok(SB_OUT), tok(DIFF_OUT), _full(wts['wb'].shape), _full(wts['wo'].shape)]
        h_res = pl.pallas_call(
            _merge_kernel, grid=(b, n_tm), in_specs=mg_specs, out_specs=tok(d),
            out_shape=jax.ShapeDtypeStruct((b, p_len, d), jnp.float32),
            compiler_params=cparams2, name="merge_out")(*mg_in)

        g2 = _row(ln2_g[layer])
        h_res = pl.pallas_call(
            functools.partial(_ffn_kernel, n_chunks=4), grid=(b, n_tm),
            in_specs=[tok(d), _full(g2.shape), _full(wts['w1'].shape), _full(wts['w2'].shape)],
            out_specs=tok(d), out_shape=jax.ShapeDtypeStruct((b, p_len, d), jnp.float32),
            compiler_params=cparams2, name="ffn")(h_res, g2, wts['w1'], wts['w2'])

    return h_res[:, PAD + N_META:]
```

```python
import functools
import math

import jax
import jax.numpy as jnp
from jax import lax
from jax.experimental import pallas as pl
from jax.experimental.pallas import tpu as pltpu

D_MODEL = 1024
DEPTH = 2
N_META = 16
BLOCK = 128
PAD = BLOCK - N_META
ROPE_THETA = 10000.0
NORM_EPS = 1e-6
MASK_VALUE = -1e30

MLA_HEADS = 8
MLA_Q_LORA = 256
MLA_KV_LORA = 128
MLA_NOPE = 64
MLA_ROPE = 32
MLA_V = 64
MLA_QK = MLA_NOPE + MLA_ROPE
SB_HEADS = 8
SB_HEAD_DIM = 64
DIFF_HEADS = 4
DIFF_HEAD_DIM = 64
DIFF_V_DIM = 2 * DIFF_HEAD_DIM
N_BRANCHES = 3
MLA_OUT = MLA_HEADS * MLA_V
SB_OUT = SB_HEADS * SB_HEAD_DIM
DIFF_QK = DIFF_HEADS * 2 * DIFF_HEAD_DIM
DIFF_OUT = DIFF_HEADS * DIFF_V_DIM
D_FF = 4 * D_MODEL
IN_SIZES = (MLA_Q_LORA, MLA_KV_LORA, MLA_ROPE, SB_OUT, SB_OUT, SB_OUT,
            DIFF_QK, DIFF_QK, DIFF_OUT, N_BRANCHES * D_MODEL)

LANES = 128
HALF_LANES = LANES // 2
MXU_DIM = 256
CHAIN = 128

LOG2E = math.log2(math.e)
VMEM_LIMIT_BYTES = 56 * 1024 * 1024
STICK_SKIP_LOG2 = 110.0 * LOG2E
STICK_HEAD_BLOCKS = 3

_NT = (((1,), (1,)), ((), ()))


def _tiles(p_len):
    assert p_len % 528 == 0 and (p_len - BLOCK) % (2 * MXU_DIM) == 0, p_len
    return dict(tm=528, t_soft=2 * MXU_DIM, t_sb=BLOCK)


def _rms(x, g, n):
    ss = jnp.sum(x * x, axis=-1, keepdims=True) * (1.0 / n)
    return (x * lax.rsqrt(ss + NORM_EPS)) * g


def _rope_lanes(x, cos, sina, sinb, half):
    return x * cos + pltpu.roll(x, LANES - half, 1) * sina + pltpu.roll(x, half, 1) * sinb


def _bdot(a, b):
    return jnp.dot(a, b, preferred_element_type=jnp.float32)


def _dot_nt(a, b):
    return lax.dot_general(a, b, _NT, preferred_element_type=jnp.float32)


def _pre_kernel(x_ref, g1_ref, wlat_ref, gcq_ref, gckv_ref, wuq_ref, wuqp_ref, wk_ref, wv_ref,
                c1q_ref, c2q_ref, c1k_ref, c2k_ref,
                wsb_ref, wdf_ref, gdq_ref, gdk_ref, cosc_ref, sinac_ref, sinbc_ref,
                qa_ref, ka_ref, va_ref, qb_ref, kb_ref, vb_ref, qc_ref, kc_ref, vc_ref):
    bf = jnp.bfloat16
    x = x_ref[...]
    h = _rms(x, g1_ref[...], D_MODEL).astype(bf)

    df = _bdot(h, wdf_ref[...])
    cosc, sinac, sinbc = cosc_ref[...], sinac_ref[...], sinbc_ref[...]
    lane = lax.broadcasted_iota(jnp.int32, (1, LANES), 1)
    lo = lane < HALF_LANES
    c_scale = (DIFF_HEAD_DIM ** -0.5) * LOG2E

    def half_norm(t, g):
        sq = t * t
        ss_lo = jnp.sum(jnp.where(lo, sq, 0.0), axis=-1, keepdims=True)
        ss_hi = jnp.sum(jnp.where(lo, 0.0, sq), axis=-1, keepdims=True)
        r = jnp.where(lo, lax.rsqrt(ss_lo * (1.0 / DIFF_HEAD_DIM) + NORM_EPS),
                      lax.rsqrt(ss_hi * (1.0 / DIFF_HEAD_DIM) + NORM_EPS))
        return (t * r) * g

    for hh in range(DIFF_HEADS):
        sl = slice(hh * LANES, (hh + 1) * LANES)
        qh = _rope_lanes(half_norm(df[:, sl], gdq_ref[...]), cosc, sinac, sinbc, DIFF_HEAD_DIM // 2)
        qc_ref[:, sl] = (qh * c_scale).astype(bf)
        sk = slice(DIFF_QK + hh * LANES, DIFF_QK + (hh + 1) * LANES)
        kh = _rope_lanes(half_norm(df[:, sk], gdk_ref[...]), cosc, sinac, sinbc, DIFF_HEAD_DIM // 2)
        kc_ref[:, sl] = kh.astype(bf)
    vc_ref[...] = df[:, 2 * DIFF_QK:].astype(bf)

    lat = _bdot(h, wlat_ref[...])
    cq = _rms(lat[:, :MLA_Q_LORA], gcq_ref[...], MLA_Q_LORA).astype(bf)
    ckv = _rms(lat[:, MLA_Q_LORA:MLA_Q_LORA + MLA_KV_LORA], gckv_ref[...], MLA_KV_LORA).astype(bf)
    kr_off = MLA_Q_LORA + MLA_KV_LORA
    kr_block = lat[:, kr_off:kr_off + LANES]
    krp_c2 = lat[:, kr_off + LANES:] * c2k_ref[...]
    qf = _bdot(cq, wuq_ref[...])
    qpf = _bdot(cq, wuqp_ref[...])
    kf = _bdot(ckv, wk_ref[...])
    va_ref[...] = _bdot(ckv, wv_ref[...]).astype(bf)
    c1q, c2q, c1k = c1q_ref[...], c2q_ref[...], c1k_ref[...]

    def inv_rms(t):
        return lax.rsqrt(jnp.sum(t * t, axis=-1, keepdims=True) * (1.0 / MLA_QK) + NORM_EPS)

    for hh in range(MLA_HEADS):
        sl = slice(hh * LANES, (hh + 1) * LANES)
        qx = qf[:, sl]
        qa_ref[:, sl] = ((qx * c1q + qpf[:, sl] * c2q) * inv_rms(qx)).astype(bf)
        kx = kf[:, sl] + kr_block
        ka_ref[:, sl] = ((kx * c1k + krp_c2) * inv_rms(kx)).astype(bf)

    sb = _bdot(h, wsb_ref[...])
    qb_ref[...] = (sb[:, :SB_OUT] * ((SB_HEAD_DIM ** -0.5) * LOG2E)).astype(bf)
    kb_ref[...] = sb[:, SB_OUT:2 * SB_OUT].astype(bf)
    vb_ref[...] = sb[:, 2 * SB_OUT:].astype(bf)


def _softmax2_kernel(*refs, mode, p_len, tile, lam_init=None):
    if mode == 'mla':
        q_ref, k_ref, v_ref, o_ref, m_scr, acc_scr, s0_scr, s1_scr, mx0_scr, mx1_scr, vx_scr = refs
    else:
        q_ref, k_ref, v_ref, lam_ref, g_ref, o_ref, m_scr, acc_scr, s0_scr, s1_scr, mx0_scr, mx1_scr, vx_scr = refs
    bf = jnp.bfloat16
    kb = MXU_DIM
    assert tile == 2 * kb, (tile, kb)
    n_tiles = (p_len - BLOCK) // tile
    n_kb = (p_len - BLOCK) // kb
    n_half = tile // CHAIN
    n_chain = 2 * n_half
    lane = lax.broadcasted_iota(jnp.int32, (1, LANES), 1)
    lo = lane < HALF_LANES

    if mode == 'diff':
        lp = lam_ref[...]
        lam = (jnp.exp(jnp.sum(lp[0:1] * lp[1:2], axis=-1, keepdims=True))
               - jnp.exp(jnp.sum(lp[2:3] * lp[3:4], axis=-1, keepdims=True)) + lam_init)

    def load_q(r0, n):
        rows = pl.ds(r0, n)
        if mode == 'mla':
            return q_ref[rows, :LANES], q_ref[rows, LANES:]
        q = q_ref[rows, :]
        return q, q

    def k_block(r0, n, m):
        rows = pl.ds(r0, n)
        if mode == 'mla':
            return k_ref[rows, m * LANES:(m + 1) * LANES]
        k = k_ref[rows, :]
        return jnp.where(lo if m == 0 else jnp.logical_not(lo), k, jnp.zeros_like(k))

    def combine(o0, o1):
        if mode == 'mla':
            return jnp.where(lo, o0, o1)
        return _rms(o0 - lam * o1, g_ref[...], DIFF_V_DIM) * (1.0 - lam_init)

    row = lax.broadcasted_iota(jnp.int32, (CHAIN, 1), 0)
    col0 = lax.broadcasted_iota(jnp.int32, (1, BLOCK), 1)
    meta_ok = col0 >= PAD

    q_meta = load_q(0, BLOCK)
    v_meta = v_ref[pl.ds(0, BLOCK), :]
    o_meta = []
    for m in range(2):
        s = jnp.where((col0 <= row) & meta_ok, _dot_nt(q_meta[m], k_block(0, BLOCK, m)), MASK_VALUE)
        p = jnp.exp2(s - jnp.max(s, axis=-1, keepdims=True))
        o_meta.append(_bdot(p.astype(bf), v_meta) / jnp.sum(p, axis=-1, keepdims=True))
    o_ref[pl.ds(0, BLOCK), :] = combine(*o_meta).astype(bf)

    col = lax.broadcasted_iota(jnp.int32, (1, kb), 1)
    diag = [[col + d * kb <= row + (c % n_half) * CHAIN for c in range(n_chain)] for d in range(2)]

    def key_row(j):
        return pl.multiple_of(BLOCK + j * kb, BLOCK)

    def qk_into(bufs, qs, r0, masks, h_from=0):
        s_scr, mx_scr = bufs
        for m in range(2):
            s = _dot_nt(qs[m][h_from * CHAIN:], k_block(r0, kb, m))
            for h in range(h_from, n_half):
                c = m * n_half + h
                rs = slice(c * CHAIN, (c + 1) * CHAIN)
                s_c = s[(h - h_from) * CHAIN:(h - h_from + 1) * CHAIN]
                if masks is not None:
                    s_c = jnp.where(masks[c], s_c, MASK_VALUE)
                s_scr[rs, :] = s_c
                mx_scr[rs, :] = jnp.broadcast_to(jnp.max(s_c, axis=-1, keepdims=True), (CHAIN, LANES))

    v_maps = 2 if mode == 'mla' else 1

    def v_ext(vblk, m):
        if mode == 'diff':
            return jnp.concatenate([vblk, jnp.ones_like(vblk)], axis=1)
        mine = lo if m == 0 else jnp.logical_not(lo)
        ones = jnp.broadcast_to(jnp.where(mine, 1.0, 0.0).astype(bf), vblk.shape)
        return jnp.concatenate([jnp.where(mine, vblk, jnp.zeros_like(vblk)), ones], axis=1)

    vx_meta_base = n_kb * v_maps * kb
    for m in range(v_maps):
        vx_scr[pl.ds(vx_meta_base + m * BLOCK, BLOCK), :] = v_ext(v_meta, m)

    def build_vx(j, c_):
        vblk = v_ref[pl.ds(key_row(j), kb), :]
        base = pl.multiple_of(j * v_maps * kb, kb)
        for m in range(v_maps):
            vx_scr[pl.ds(base + m * kb, kb), :] = v_ext(vblk, m)
        return c_

    lax.fori_loop(0, n_kb, build_vx, 0)

    def update(get_s, get_max, vx_base, n, h_from=0):
        rhs = vx_scr[pl.ds(vx_base, v_maps * n), :]
        for h in range(h_from, n_half):
            parts, alphas = [], []
            for m in range(2):
                c = m * n_half + h
                rs = slice(c * CHAIN, (c + 1) * CHAIN)
                s = get_s(c)
                m_run = m_scr[rs, :]
                m_new = jnp.maximum(m_run, get_max(c))
                alpha = jnp.exp2(m_run - m_new)
                m_scr[rs, :] = m_new
                ps = [jnp.exp2(s[:, k * LANES:(k + 1) * LANES] - m_new).astype(bf) for k in range(n // LANES)]
                if mode == 'diff':
                    p = ps[0] if len(ps) == 1 else jnp.concatenate(ps, axis=1)
                    acc_scr[rs, :] = jnp.concatenate([alpha, alpha], axis=1) * acc_scr[rs, :] + _bdot(p, rhs)
                parts += ps
                alphas.append(alpha)
            if mode == 'mla':
                both = jnp.where(lo, alphas[0], alphas[1])
                ah = slice(h * CHAIN, (h + 1) * CHAIN)
                acc_scr[ah, :] = (jnp.concatenate([both, both], axis=1) * acc_scr[ah, :]
                                  + _bdot(jnp.concatenate(parts, axis=1), rhs))

    def update_from(bufs, vx_base, h_from=0):
        s_scr, mx_scr = bufs
        update(lambda c: s_scr[c * CHAIN:(c + 1) * CHAIN, :], lambda c: mx_scr[c * CHAIN:(c + 1) * CHAIN, :],
               vx_base, kb, h_from)

    def vx_row(j):
        return pl.multiple_of(j * v_maps * kb, kb)

    buf0, buf1 = (s0_scr, mx0_scr), (s1_scr, mx1_scr)

    def q_tile(i, carry):
        r0 = pl.multiple_of(BLOCK + i * tile, BLOCK)
        qs = load_q(r0, tile)
        m_scr[...] = jnp.full(m_scr.shape, MASK_VALUE, jnp.float32)
        acc_scr[...] = jnp.zeros(acc_scr.shape, jnp.float32)
        first = i == 0
        qk_into(buf0, qs, key_row(0), [col <= row + (c % n_half) * CHAIN + i * tile for c in range(n_chain)])

        s_meta = [jnp.where(meta_ok, _dot_nt(qs[m], k_block(0, BLOCK, m)), MASK_VALUE) for m in range(2)]

        def meta_s(c):
            return s_meta[c // n_half][(c % n_half) * CHAIN:(c % n_half + 1) * CHAIN]

        update(meta_s, lambda c: jnp.max(meta_s(c), axis=-1, keepdims=True), vx_meta_base, BLOCK)

        def pair(t, c_):
            j = 2 * t
            qk_into(buf1, qs, key_row(j + 1), None)
            update_from(buf0, vx_row(j))
            qk_into(buf0, qs, key_row(j + 2), None)
            update_from(buf1, vx_row(j + 1))
            return c_

        lax.fori_loop(0, jnp.maximum(i - 1, 0), pair, 0)
        jd = 2 * i

        @pl.when(jnp.logical_not(first))
        def _():
            qk_into(buf1, qs, key_row(jd - 1), None)
            update_from(buf0, vx_row(jd - 2))
            qk_into(buf0, qs, key_row(jd), diag[0])
            update_from(buf1, vx_row(jd - 1))

        h_diag1 = kb // CHAIN
        qk_into(buf1, qs, key_row(jd + 1), diag[1], h_diag1)
        update_from(buf0, vx_row(jd))
        update_from(buf1, vx_row(jd + 1), h_diag1)

        for h in range(n_half):
            a = acc_scr[h * CHAIN:(h + 1) * CHAIN, :]
            if mode == 'mla':
                res = a[:, :LANES] / a[:, LANES:]
            else:
                a1 = acc_scr[(n_half + h) * CHAIN:(n_half + h + 1) * CHAIN, :]
                res = combine(a[:, :LANES] / a[:, LANES:], a1[:, :LANES] / a1[:, LANES:])
            o_ref[pl.ds(r0 + h * CHAIN, CHAIN), :] = res.astype(bf)
        return carry

    lax.fori_loop(0, n_tiles, q_tile, 0)


def _stick_kernel(q_ref, k_ref, v_ref, tri_ref, o_ref, carry_scr, acc_scr, *, tq, tk):
    bf = jnp.bfloat16
    i = pl.program_id(1)
    n_blocks = q_ref.shape[-1] // LANES
    lane = lax.broadcasted_iota(jnp.int32, (1, LANES), 1)
    lo = lane < HALF_LANES
    qs = []
    for blk in range(n_blocks):
        q = q_ref[:, blk * LANES:(blk + 1) * LANES]
        zero = jnp.zeros_like(q)
        qs.append(jnp.concatenate([jnp.where(lo, q, zero), jnp.where(lo, zero, q)], axis=0))
    tri2 = tri_ref[...]
    qpos = i * tq + (lax.broadcasted_iota(jnp.int32, (2 * tq, 1), 0) & (tq - 1))
    carry_scr[...] = jnp.zeros(carry_scr.shape, jnp.float32)
    acc_scr[...] = jnp.zeros(acc_scr.shape, jnp.float32)

    def blocks_step(js, flags):
        rows = [pl.ds(pl.multiple_of(j * tk, tk), tk) for j in js]
        masks = []
        for j, masked in zip(js, flags):
            kpos = j * tk + lax.broadcasted_iota(jnp.int32, (1, tk), 1)
            masks.append(((kpos < qpos) & (kpos >= PAD)) if masked else None)
        cols = [slice(blk * LANES, (blk + 1) * LANES) for blk in range(n_blocks)]
        zs = [[_dot_nt(qs[blk], k_ref[r, cols[blk]]) for blk in range(n_blocks)] for r in rows]
        cums, ksums = [], []
        for b in range(len(js)):
            cums.append([])
            ksums.append([])
            for blk in range(n_blocks):
                z = zs[b][blk]
                keep = -(jnp.maximum(z, 0.0) + jnp.log2(1.0 + jnp.exp2(-jnp.abs(z))))
                if masks[b] is not None:
                    keep = jnp.where(masks[b], keep, 0.0)
                hi = keep.astype(bf)
                lw = (keep - hi.astype(jnp.float32)).astype(bf)
                cums[b].append(_bdot(jnp.concatenate([hi, lw], axis=1), tri2))
                ksums[b].append(jnp.sum(keep, axis=-1, keepdims=True))
        worst = None
        for blk in range(n_blocks):
            rs = slice(2 * blk * tq, 2 * (blk + 1) * tq)
            carry = carry_scr[rs, :]
            pv = None
            for b in range(len(js)):
                a = jnp.exp2(zs[b][blk] + (carry + cums[b][blk]))
                if masks[b] is not None:
                    a = jnp.where(masks[b], a, 0.0)
                d = _bdot(a.astype(bf), v_ref[rows[b], cols[blk]])
                pv = d if pv is None else pv + d
                carry = carry + ksums[b][blk]
            carry_scr[rs, :] = carry
            acc_scr[rs, :] += pv
            worst = carry if worst is None else jnp.maximum(worst, carry)
        return jnp.max(worst)

    def cond(st):
        j, worst = st
        return (j >= 0) & (worst > -STICK_SKIP_LOG2)

    def body(st):
        j, _ = st
        worst = lax.cond((j == i) | (j == 0), lambda: blocks_step([j], [True]), lambda: blocks_step([j], [False]))
        return j - 1, worst

    head = STICK_HEAD_BLOCKS
    start = lax.cond(i >= head,
                     lambda: (i - head, blocks_step([i - b for b in range(head)], [True] + [False] * (head - 1))),
                     lambda: (i, jnp.float32(0.0)))
    lax.while_loop(cond, body, start)
    for blk in range(n_blocks):
        r0 = 2 * blk * tq
        o_ref[:, blk * LANES:(blk + 1) * LANES] = jnp.where(
            lo, acc_scr[r0:r0 + tq, :], acc_scr[r0 + tq:r0 + 2 * tq, :]).astype(bf)


def _post_kernel(x_ref, g1_ref, wg_ref, gb_ref, oa_ref, ob_ref, oc_ref, wb_ref, wo_ref,
                 g2_ref, w1_ref, w2_ref, y_ref, *, n_chunks):
    bf = jnp.bfloat16
    x = x_ref[...]
    h = _rms(x, g1_ref[...], D_MODEL).astype(bf)
    outs = (oa_ref, ob_ref, oc_ref)
    merged = None
    for br in range(N_BRANCHES):
        cs = slice(br * D_MODEL, (br + 1) * D_MODEL)
        logits = _bdot(h, wg_ref[:, cs]) + gb_ref[:, cs]
        gate = 1.0 / (1.0 + jnp.exp(-logits))
        rs = slice(br * MLA_OUT, (br + 1) * MLA_OUT)
        term = gate * _bdot(outs[br][...], wb_ref[rs, :])
        merged = term if merged is None else merged + term
    x = x + _bdot(merged.astype(bf), wo_ref[...])

    h2 = _rms(x, g2_ref[...], D_MODEL).astype(bf)
    ck = D_FF // n_chunks
    acc = x
    for c in range(n_chunks):
        cs = slice(c * ck, (c + 1) * ck)
        f = jnp.maximum(_bdot(h2, w1_ref[:, cs]), 0.0)
        acc = acc + _bdot((f * f).astype(bf), w2_ref[cs, :])
    y_ref[...] = acc


def _rope_tables(p_len, d, blocks):
    half = d // 2
    pos = jnp.maximum(jnp.arange(p_len) - PAD, 0)
    inv_freq = jnp.exp(-math.log(ROPE_THETA) * (2.0 * jnp.arange(half, dtype=jnp.float32) / d))
    ang = pos.astype(jnp.float32)[:, None] * inv_freq[None, :]
    cos, sin = jnp.cos(ang), jnp.sin(ang)
    zero = jnp.zeros_like(sin)
    cos_p, sina_p, sinb_p, at = [], [], [], 0
    for off in blocks:
        gap = off - at
        cos_p += [jnp.ones((p_len, gap), jnp.float32), cos, cos]
        sina_p += [jnp.zeros((p_len, gap), jnp.float32), -sin, zero]
        sinb_p += [jnp.zeros((p_len, gap), jnp.float32), zero, sin]
        at = off + d
    cos_p.append(jnp.ones((p_len, LANES - at), jnp.float32))
    sina_p.append(jnp.zeros((p_len, LANES - at), jnp.float32))
    sinb_p.append(jnp.zeros((p_len, LANES - at), jnp.float32))
    return tuple(jnp.concatenate(parts, axis=1) for parts in (cos_p, sina_p, sinb_p))


def _swap_halves(t):
    half = t.shape[-1] // 2
    return jnp.concatenate([t[..., half:], t[..., :half]], axis=-1)


def _row(v):
    return v.reshape(1, -1).astype(jnp.float32)


def _full(shape):
    return pl.BlockSpec(shape, lambda *_: (0,) * len(shape), pipeline_mode=pl.Buffered(1))


def _layer_weights(layer, w_in, mla_w_uq, mla_w_ukv, w_branch, w_out, w_ff1, w_ff2):
    bf = jnp.bfloat16
    w = w_in[layer]
    offs = [0]
    for s in IN_SIZES:
        offs.append(offs[-1] + s)
    d = w.shape[0]
    w_kr = w[:, offs[2]:offs[3]]
    w_lat = jnp.concatenate([
        w[:, offs[0]:offs[2]],
        jnp.zeros((d, MLA_NOPE), w.dtype), w_kr,
        jnp.zeros((d, LANES - MLA_QK), w.dtype),
        jnp.zeros((d, MLA_NOPE), w.dtype), _swap_halves(w_kr),
        jnp.zeros((d, LANES - MLA_QK), w.dtype)], axis=1).astype(bf)
    w_sb = w[:, offs[3]:offs[6]].astype(bf)
    w_df = w[:, offs[6]:offs[9]].astype(bf)
    w_gate = w[:, offs[9]:offs[10]].astype(bf)
    wuq = mla_w_uq[layer].reshape(MLA_Q_LORA, MLA_HEADS, MLA_QK)
    wuqp = jnp.concatenate([wuq[..., :MLA_NOPE], _swap_halves(wuq[..., MLA_NOPE:])], axis=-1)
    wuq = jnp.pad(wuq, ((0, 0), (0, 0), (0, LANES - MLA_QK))).reshape(MLA_Q_LORA, MLA_HEADS * LANES).astype(bf)
    wuqp = jnp.pad(wuqp, ((0, 0), (0, 0), (0, LANES - MLA_QK))).reshape(MLA_Q_LORA, MLA_HEADS * LANES).astype(bf)
    wukv = mla_w_ukv[layer].reshape(MLA_KV_LORA, MLA_HEADS, MLA_NOPE + MLA_V)
    wk = jnp.pad(wukv[:, :, :MLA_NOPE], ((0, 0), (0, 0), (0, LANES - MLA_NOPE)))
    wk = wk.reshape(MLA_KV_LORA, MLA_HEADS * LANES).astype(bf)
    wv = wukv[:, :, MLA_NOPE:].reshape(MLA_KV_LORA, MLA_OUT).astype(bf)
    return dict(w_lat=w_lat, w_sb=w_sb, w_df=w_df, w_gate=w_gate, wuq=wuq, wuqp=wuqp, wk=wk, wv=wv,
                wb=w_branch[layer].astype(bf), wo=w_out[layer].astype(bf),
                w1=w_ff1[layer].astype(bf), w2=w_ff2[layer].astype(bf))


def kernel(x, meta_tokens, ln1_g, w_in, mla_cq_norm_g, mla_ckv_norm_g, mla_w_uq, mla_w_ukv,
           mla_q_norm_g, mla_k_norm_g, diff_q_norm_g, diff_k_norm_g, diff_lambda,
           diff_out_norm_g, gate_b, w_branch, w_out, ln2_g, w_ff1, w_ff2):
    b, seq, d = x.shape
    assert d == D_MODEL
    bf = jnp.bfloat16
    pad = jnp.zeros((b, PAD, d), x.dtype)
    meta = jnp.broadcast_to(meta_tokens.astype(x.dtype)[None], (b, N_META, d))
    h_res = jnp.concatenate([pad, meta, x], axis=1)
    p_len = h_res.shape[1]
    t = _tiles(p_len)
    tm, tqs, tqb = t['tm'], t['t_soft'], t['t_sb']
    n_tm = p_len // tm

    tab_a = _rope_tables(p_len, MLA_ROPE, [MLA_NOPE])
    tab_c = _rope_tables(p_len, DIFF_HEAD_DIM, [0, HALF_LANES])
    tri = (jnp.arange(tqb)[:, None] >= jnp.arange(tqb)[None, :]).astype(bf)
    tri2 = jnp.concatenate([tri, tri], axis=0)

    cparams2 = pltpu.CompilerParams(dimension_semantics=("parallel", "parallel"),
                                    vmem_limit_bytes=VMEM_LIMIT_BYTES)
    cparams2s = pltpu.CompilerParams(dimension_semantics=("parallel", "arbitrary"),
                                     vmem_limit_bytes=VMEM_LIMIT_BYTES)

    def tok(width):
        return pl.BlockSpec((None, tm, width), lambda bi, ti: (bi, ti, 0))

    def tab():
        return pl.BlockSpec((tm, LANES), lambda bi, ti: (ti, 0))

    def act(width):
        return jax.ShapeDtypeStruct((b, p_len, width), bf)

    def soft_scratch(v_maps):
        return [pltpu.VMEM((2 * tqs, LANES), jnp.float32),
                pltpu.VMEM((2 * tqs, 2 * LANES), jnp.float32),
                pltpu.VMEM((2 * tqs, MXU_DIM), jnp.float32), pltpu.VMEM((2 * tqs, MXU_DIM), jnp.float32),
                pltpu.VMEM((2 * tqs, LANES), jnp.float32), pltpu.VMEM((2 * tqs, LANES), jnp.float32),
                pltpu.VMEM((v_maps * p_len, 2 * LANES), bf)]

    def seq_block(width):
        return pl.BlockSpec((None, p_len, width), lambda bi, ci: (bi, 0, ci))

    for layer in range(DEPTH):
        wts = _layer_weights(layer, w_in, mla_w_uq, mla_w_ukv, w_branch, w_out, w_ff1, w_ff2)
        g1 = _row(ln1_g[layer])
        gdq = _row(jnp.tile(diff_q_norm_g[layer], 2))
        gdk = _row(jnp.tile(diff_k_norm_g[layer], 2))

        def mla_tables(gain, scale):
            g = gain.astype(jnp.float32)
            g_own = _row(jnp.pad(g, (0, LANES - MLA_QK)))
            g_partner = _row(jnp.pad(jnp.concatenate([g[:MLA_NOPE], _swap_halves(g[MLA_NOPE:])]),
                                     (0, LANES - MLA_QK)))
            return tab_a[0] * (g_own * scale), (tab_a[1] + tab_a[2]) * (g_partner * scale)

        c1q, c2q = mla_tables(mla_q_norm_g[layer], (MLA_QK ** -0.5) * LOG2E)
        c1k, c2k = mla_tables(mla_k_norm_g[layer], 1.0)

        pre_in = [h_res, g1, wts['w_lat'], _row(mla_cq_norm_g[layer]), _row(mla_ckv_norm_g[layer]),
                  wts['wuq'], wts['wuqp'], wts['wk'], wts['wv'], c1q, c2q, c1k, c2k,
                  wts['w_sb'], wts['w_df'], gdq, gdk, *tab_c]
        pre_specs = [tok(d)] + [_full(a.shape) for a in pre_in[1:9]] + [tab()] * 4 \
            + [_full(a.shape) for a in pre_in[13:17]] + [tab()] * 3
        widths = (MLA_HEADS * LANES, MLA_HEADS * LANES, MLA_OUT, SB_OUT, SB_OUT, SB_OUT,
                  DIFF_QK, DIFF_QK, DIFF_OUT)
        qa, ka, va, qb, kb, vb, qc, kc, vc = pl.pallas_call(
            _pre_kernel, grid=(b, n_tm), in_specs=pre_specs,
            out_specs=[tok(wd) for wd in widths], out_shape=[act(wd) for wd in widths],
            compiler_params=cparams2, name="pre_tokens")(*pre_in)

        out_a = pl.pallas_call(
            functools.partial(_softmax2_kernel, mode='mla', p_len=p_len, tile=tqs),
            grid=(b, MLA_HEADS // 2),
            in_specs=[seq_block(2 * LANES), seq_block(2 * LANES), seq_block(LANES)],
            out_specs=seq_block(LANES), scratch_shapes=soft_scratch(2),
            out_shape=act(MLA_OUT), compiler_params=cparams2, name="attn_mla")(qa, ka, va)

        n_qb = p_len // tqb
        out_b = pl.pallas_call(
            functools.partial(_stick_kernel, tq=tqb, tk=tqb),
            grid=(b, n_qb),
            in_specs=[pl.BlockSpec((None, tqb, SB_OUT), lambda bi, qi: (bi, qi, 0)),
                      pl.BlockSpec((None, p_len, SB_OUT), lambda bi, qi: (bi, 0, 0)),
                      pl.BlockSpec((None, p_len, SB_OUT), lambda bi, qi: (bi, 0, 0)),
                      _full(tri2.shape)],
            out_specs=pl.BlockSpec((None, tqb, SB_OUT), lambda bi, qi: (bi, qi, 0)),
            scratch_shapes=[pltpu.VMEM((SB_HEADS * tqb, LANES), jnp.float32)] * 2,
            out_shape=act(SB_OUT), compiler_params=cparams2s, name="attn_stick")(qb, kb, vb, tri2)

        lam_init = 0.8 - 0.6 * math.exp(-0.3 * layer)
        g_c = _row(diff_out_norm_g[layer])
        out_c = pl.pallas_call(
            functools.partial(_softmax2_kernel, mode='diff', p_len=p_len, tile=tqs, lam_init=lam_init),
            grid=(b, DIFF_HEADS),
            in_specs=[seq_block(LANES), seq_block(LANES), seq_block(LANES),
                      _full((4, DIFF_HEAD_DIM)), _full((1, LANES))],
            out_specs=seq_block(LANES), scratch_shapes=soft_scratch(1),
            out_shape=act(DIFF_OUT), compiler_params=cparams2, name="attn_diff")(
                qc, kc, vc, diff_lambda[layer].astype(jnp.float32), g_c)

        g2 = _row(ln2_g[layer])
        post_in = [h_res, g1, wts['w_gate'], _row(gate_b[layer]), out_a, out_b, out_c, wts['wb'], wts['wo'],
                   g2, wts['w1'], wts['w2']]
        post_specs = [tok(d)] + [_full(a.shape) for a in post_in[1:4]] \
            + [tok(MLA_OUT), tok(SB_OUT), tok(DIFF_OUT)] + [_full(a.shape) for a in post_in[7:]]
        h_res = pl.pallas_call(
            functools.partial(_post_kernel, n_chunks=4), grid=(b, n_tm), in_specs=post_specs, out_specs=tok(d),
            out_shape=jax.ShapeDtypeStruct((b, p_len, d), jnp.float32),
            compiler_params=cparams2, name="merge_ffn")(*post_in)

    return h_res[:, PAD + N_META:]
```

```python
import functools
import math

import jax
import jax.numpy as jnp
from jax import lax
from jax.experimental import pallas as pl
from jax.experimental.pallas import tpu as pltpu

D_MODEL = 1024
DEPTH = 2
N_META = 16
BLOCK = 128
PAD = BLOCK - N_META
ROPE_THETA = 10000.0
NORM_EPS = 1e-6
MASK_VALUE = -1e30

MLA_HEADS = 8
MLA_Q_LORA = 256
MLA_KV_LORA = 128
MLA_NOPE = 64
MLA_ROPE = 32
MLA_V = 64
MLA_QK = MLA_NOPE + MLA_ROPE
SB_HEADS = 8
SB_HEAD_DIM = 64
DIFF_HEADS = 4
DIFF_HEAD_DIM = 64
DIFF_V_DIM = 2 * DIFF_HEAD_DIM
N_BRANCHES = 3
MLA_OUT = MLA_HEADS * MLA_V
SB_OUT = SB_HEADS * SB_HEAD_DIM
DIFF_QK = DIFF_HEADS * 2 * DIFF_HEAD_DIM
DIFF_OUT = DIFF_HEADS * DIFF_V_DIM
D_FF = 4 * D_MODEL
IN_SIZES = (MLA_Q_LORA, MLA_KV_LORA, MLA_ROPE, SB_OUT, SB_OUT, SB_OUT,
            DIFF_QK, DIFF_QK, DIFF_OUT, N_BRANCHES * D_MODEL)

LANES = 128
HALF_LANES = LANES // 2
MXU_DIM = 256
CHAIN = 128

LOG2E = math.log2(math.e)
VMEM_LIMIT_BYTES = 56 * 1024 * 1024
STICK_SKIP_LOG2 = 110.0 * LOG2E
STICK_HEAD_BLOCKS = 3

_NT = (((1,), (1,)), ((), ()))


def _tiles(p_len):
    assert p_len % 528 == 0 and (p_len - BLOCK) % (2 * MXU_DIM) == 0, p_len
    return dict(tm=528, t_soft=2 * MXU_DIM, t_sb=BLOCK)


def _rms(x, g, n):
    ss = jnp.sum(x * x, axis=-1, keepdims=True) * (1.0 / n)
    return (x * lax.rsqrt(ss + NORM_EPS)) * g


def _rope_lanes(x, cos, sina, sinb, half):
    return x * cos + pltpu.roll(x, LANES - half, 1) * sina + pltpu.roll(x, half, 1) * sinb


def _bdot(a, b):
    return jnp.dot(a, b, preferred_element_type=jnp.float32)


def _dot_nt(a, b):
    return lax.dot_general(a, b, _NT, preferred_element_type=jnp.float32)


def _pre_kernel(x_ref, g1_ref, wlat_ref, gcq_ref, gckv_ref, wuq_ref, wuqp_ref, wk_ref, wv_ref,
                c1q_ref, c2q_ref, c1k_ref, c2k_ref,
                wsb_ref, wdf_ref, gdq_ref, gdk_ref, cosc_ref, sinac_ref, sinbc_ref,
                qa_ref, ka_ref, va_ref, qb_ref, kb_ref, vb_ref, qc_ref, kc_ref, vc_ref):
    bf = jnp.bfloat16
    x = x_ref[...]
    h = _rms(x, g1_ref[...], D_MODEL).astype(bf)

    df = _bdot(h, wdf_ref[...])
    cosc, sinac, sinbc = cosc_ref[...], sinac_ref[...], sinbc_ref[...]
    lane = lax.broadcasted_iota(jnp.int32, (1, LANES), 1)
    lo = lane < HALF_LANES
    c_scale = (DIFF_HEAD_DIM ** -0.5) * LOG2E

    def half_norm(t, g):
        sq = t * t
        ss_lo = jnp.sum(jnp.where(lo, sq, 0.0), axis=-1, keepdims=True)
        ss_hi = jnp.sum(jnp.where(lo, 0.0, sq), axis=-1, keepdims=True)
        r = jnp.where(lo, lax.rsqrt(ss_lo * (1.0 / DIFF_HEAD_DIM) + NORM_EPS),
                      lax.rsqrt(ss_hi * (1.0 / DIFF_HEAD_DIM) + NORM_EPS))
        return (t * r) * g

    for hh in range(DIFF_HEADS):
        sl = slice(hh * LANES, (hh + 1) * LANES)
        qh = _rope_lanes(half_norm(df[:, sl], gdq_ref[...]), cosc, sinac, sinbc, DIFF_HEAD_DIM // 2)
        qc_ref[:, sl] = (qh * c_scale).astype(bf)
        sk = slice(DIFF_QK + hh * LANES, DIFF_QK + (hh + 1) * LANES)
        kh = _rope_lanes(half_norm(df[:, sk], gdk_ref[...]), cosc, sinac, sinbc, DIFF_HEAD_DIM // 2)
        kc_ref[:, sl] = kh.astype(bf)
    vc_ref[...] = df[:, 2 * DIFF_QK:].astype(bf)

    lat = _bdot(h, wlat_ref[...])
    cq = _rms(lat[:, :MLA_Q_LORA], gcq_ref[...], MLA_Q_LORA).astype(bf)
    ckv = _rms(lat[:, MLA_Q_LORA:MLA_Q_LORA + MLA_KV_LORA], gckv_ref[...], MLA_KV_LORA).astype(bf)
    kr_off = MLA_Q_LORA + MLA_KV_LORA
    kr_block = lat[:, kr_off:kr_off + LANES]
    krp_c2 = lat[:, kr_off + LANES:] * c2k_ref[...]
    qf = _bdot(cq, wuq_ref[...])
    qpf = _bdot(cq, wuqp_ref[...])
    kf = _bdot(ckv, wk_ref[...])
    va_ref[...] = _bdot(ckv, wv_ref[...]).astype(bf)
    c1q, c2q, c1k = c1q_ref[...], c2q_ref[...], c1k_ref[...]

    def inv_rms(t):
        return lax.rsqrt(jnp.sum(t * t, axis=-1, keepdims=True) * (1.0 / MLA_QK) + NORM_EPS)

    for hh in range(MLA_HEADS):
        sl = slice(hh * LANES, (hh + 1) * LANES)
        qx = qf[:, sl]
        qa_ref[:, sl] = ((qx * c1q + qpf[:, sl] * c2q) * inv_rms(qx)).astype(bf)
        kx = kf[:, sl] + kr_block
        ka_ref[:, sl] = ((kx * c1k + krp_c2) * inv_rms(kx)).astype(bf)

    sb = _bdot(h, wsb_ref[...])
    qb_ref[...] = (sb[:, :SB_OUT] * ((SB_HEAD_DIM ** -0.5) * LOG2E)).astype(bf)
    kb_ref[...] = sb[:, SB_OUT:2 * SB_OUT].astype(bf)
    vb_ref[...] = sb[:, 2 * SB_OUT:].astype(bf)


def _softmax2_kernel(*refs, mode, p_len, tile, lam_init=None):
    if mode == 'mla':
        q_ref, k_ref, v_ref, o_ref, m_scr, acc_scr, s0_scr, s1_scr, mx0_scr, mx1_scr, vx_scr = refs
    else:
        q_ref, k_ref, v_ref, lam_ref, g_ref, o_ref, m_scr, acc_scr, s0_scr, s1_scr, mx0_scr, mx1_scr, vx_scr = refs
    bf = jnp.bfloat16
    kb = MXU_DIM
    assert tile == 2 * kb, (tile, kb)
    n_tiles = (p_len - BLOCK) // tile
    n_kb = (p_len - BLOCK) // kb
    n_half = tile // CHAIN
    n_chain = 2 * n_half
    lane = lax.broadcasted_iota(jnp.int32, (1, LANES), 1)
    lo = lane < HALF_LANES

    if mode == 'diff':
        lp = lam_ref[...]
        lam = (jnp.exp(jnp.sum(lp[0:1] * lp[1:2], axis=-1, keepdims=True))
               - jnp.exp(jnp.sum(lp[2:3] * lp[3:4], axis=-1, keepdims=True)) + lam_init)

    def load_q(r0, n):
        rows = pl.ds(r0, n)
        if mode == 'mla':
            return q_ref[rows, :LANES], q_ref[rows, LANES:]
        q = q_ref[rows, :]
        return q, q

    def k_block(r0, n, m):
        rows = pl.ds(r0, n)
        if mode == 'mla':
            return k_ref[rows, m * LANES:(m + 1) * LANES]
        k = k_ref[rows, :]
        return jnp.where(lo if m == 0 else jnp.logical_not(lo), k, jnp.zeros_like(k))

    def combine(o0, o1):
        if mode == 'mla':
            return jnp.where(lo, o0, o1)
        return _rms(o0 - lam * o1, g_ref[...], DIFF_V_DIM) * (1.0 - lam_init)

    row = lax.broadcasted_iota(jnp.int32, (CHAIN, 1), 0)
    col0 = lax.broadcasted_iota(jnp.int32, (1, BLOCK), 1)
    meta_ok = col0 >= PAD

    q_meta = load_q(0, BLOCK)
    v_meta = v_ref[pl.ds(0, BLOCK), :]
    o_meta = []
    for m in range(2):
        s = jnp.where((col0 <= row) & meta_ok, _dot_nt(q_meta[m], k_block(0, BLOCK, m)), MASK_VALUE)
        p = jnp.exp2(s - jnp.max(s, axis=-1, keepdims=True))
        o_meta.append(_bdot(p.astype(bf), v_meta) / jnp.sum(p, axis=-1, keepdims=True))
    o_ref[pl.ds(0, BLOCK), :] = combine(*o_meta).astype(bf)

    col = lax.broadcasted_iota(jnp.int32, (1, kb), 1)
    diag = [[col + d * kb <= row + (c % n_half) * CHAIN for c in range(n_chain)] for d in range(2)]

    def key_row(j):
        return pl.multiple_of(BLOCK + j * kb, BLOCK)

    def qk_into(bufs, qs, r0, masks, h_from=0):
        s_scr, mx_scr = bufs
        for m in range(2):
            s = _dot_nt(qs[m][h_from * CHAIN:], k_block(r0, kb, m))
            for h in range(h_from, n_half):
                c = m * n_half + h
                rs = slice(c * CHAIN, (c + 1) * CHAIN)
                s_c = s[(h - h_from) * CHAIN:(h - h_from + 1) * CHAIN]
                if masks is not None:
                    s_c = jnp.where(masks[c], s_c, MASK_VALUE)
                s_scr[rs, :] = s_c
                mx_scr[rs, :] = jnp.broadcast_to(jnp.max(s_c, axis=-1, keepdims=True), (CHAIN, LANES))

    v_maps = 2 if mode == 'mla' else 1

    def v_ext(vblk, m):
        if mode == 'diff':
            return jnp.concatenate([vblk, jnp.ones_like(vblk)], axis=1)
        mine = lo if m == 0 else jnp.logical_not(lo)
        ones = jnp.broadcast_to(jnp.where(mine, 1.0, 0.0).astype(bf), vblk.shape)
        return jnp.concatenate([jnp.where(mine, vblk, jnp.zeros_like(vblk)), ones], axis=1)

    vx_meta_base = n_kb * v_maps * kb
    for m in range(v_maps):
        vx_scr[pl.ds(vx_meta_base + m * BLOCK, BLOCK), :] = v_ext(v_meta, m)

    def build_vx(j, c_):
        vblk = v_ref[pl.ds(key_row(j), kb), :]
        base = pl.multiple_of(j * v_maps * kb, kb)
        for m in range(v_maps):
            vx_scr[pl.ds(base + m * kb, kb), :] = v_ext(vblk, m)
        return c_

    lax.fori_loop(0, n_kb, build_vx, 0)

    def update(get_s, get_max, vx_base, n, h_from=0):
        rhs = vx_scr[pl.ds(vx_base, v_maps * n), :]
        for h in range(h_from, n_half):
            parts, alphas = [], []
            for m in range(2):
                c = m * n_half + h
                rs = slice(c * CHAIN, (c + 1) * CHAIN)
                s = get_s(c)
                m_run = m_scr[rs, :]
                m_new = jnp.maximum(m_run, get_max(c))
                alpha = jnp.exp2(m_run - m_new)
                m_scr[rs, :] = m_new
                ps = [jnp.exp2(s[:, k * LANES:(k + 1) * LANES] - m_new).astype(bf) for k in range(n // LANES)]
                if mode == 'diff':
                    p = ps[0] if len(ps) == 1 else jnp.concatenate(ps, axis=1)
                    acc_scr[rs, :] = jnp.concatenate([alpha, alpha], axis=1) * acc_scr[rs, :] + _bdot(p, rhs)
                parts += ps
                alphas.append(alpha)
            if mode == 'mla':
                both = jnp.where(lo, alphas[0], alphas[1])
                ah = slice(h * CHAIN, (h + 1) * CHAIN)
                acc_scr[ah, :] = (jnp.concatenate([both, both], axis=1) * acc_scr[ah, :]
                                  + _bdot(jnp.concatenate(parts, axis=1), rhs))

    def update_from(bufs, vx_base, h_from=0):
        s_scr, mx_scr = bufs
        update(lambda c: s_scr[c * CHAIN:(c + 1) * CHAIN, :], lambda c: mx_scr[c * CHAIN:(c + 1) * CHAIN, :],
               vx_base, kb, h_from)

    def vx_row(j):
        return pl.multiple_of(j * v_maps * kb, kb)

    buf0, buf1 = (s0_scr, mx0_scr), (s1_scr, mx1_scr)

    def q_tile(i, carry):
        r0 = pl.multiple_of(BLOCK + i * tile, BLOCK)
        qs = load_q(r0, tile)
        m_scr[...] = jnp.full(m_scr.shape, MASK_VALUE, jnp.float32)
        acc_scr[...] = jnp.zeros(acc_scr.shape, jnp.float32)
        first = i == 0
        qk_into(buf0, qs, key_row(0), [col <= row + (c % n_half) * CHAIN + i * tile for c in range(n_chain)])

        s_meta = [jnp.where(meta_ok, _dot_nt(qs[m], k_block(0, BLOCK, m)), MASK_VALUE) for m in range(2)]

        def meta_s(c):
            return s_meta[c // n_half][(c % n_half) * CHAIN:(c % n_half + 1) * CHAIN]

        update(meta_s, lambda c: jnp.max(meta_s(c), axis=-1, keepdims=True), vx_meta_base, BLOCK)

        def pair_at(j):
            qk_into(buf1, qs, key_row(j + 1), None)
            update_from(buf0, vx_row(j))
            qk_into(buf0, qs, key_row(j + 2), None)
            update_from(buf1, vx_row(j + 1))

        def quad(t, c_):
            pair_at(4 * t)
            pair_at(4 * t + 2)
            return c_

        n_pairs = jnp.maximum(i - 1, 0)
        lax.fori_loop(0, lax.shift_right_logical(n_pairs, 1), quad, 0)

        @pl.when((n_pairs & 1) == 1)
        def _():
            pair_at(2 * (n_pairs - 1))
        jd = 2 * i

        @pl.when(jnp.logical_not(first))
        def _():
            qk_into(buf1, qs, key_row(jd - 1), None)
            update_from(buf0, vx_row(jd - 2))
            qk_into(buf0, qs, key_row(jd), diag[0])
            update_from(buf1, vx_row(jd - 1))

        h_diag1 = kb // CHAIN
        qk_into(buf1, qs, key_row(jd + 1), diag[1], h_diag1)
        update_from(buf0, vx_row(jd))
        update_from(buf1, vx_row(jd + 1), h_diag1)

        for h in range(n_half):
            a = acc_scr[h * CHAIN:(h + 1) * CHAIN, :]
            if mode == 'mla':
                res = a[:, :LANES] / a[:, LANES:]
            else:
                a1 = acc_scr[(n_half + h) * CHAIN:(n_half + h + 1) * CHAIN, :]
                res = combine(a[:, :LANES] / a[:, LANES:], a1[:, :LANES] / a1[:, LANES:])
            o_ref[pl.ds(r0 + h * CHAIN, CHAIN), :] = res.astype(bf)
        return carry

    lax.fori_loop(0, n_tiles, q_tile, 0)


def _stick_kernel(q_ref, k_ref, v_ref, tri_ref, o_ref, carry_scr, acc_scr, *, tq, tk):
    bf = jnp.bfloat16
    i = pl.program_id(1)
    n_blocks = q_ref.shape[-1] // LANES
    lane = lax.broadcasted_iota(jnp.int32, (1, LANES), 1)
    lo = lane < HALF_LANES
    qs = []
    for blk in range(n_blocks):
        q = q_ref[:, blk * LANES:(blk + 1) * LANES]
        zero = jnp.zeros_like(q)
        qs.append(jnp.concatenate([jnp.where(lo, q, zero), jnp.where(lo, zero, q)], axis=0))
    tri2 = tri_ref[...]
    qpos = i * tq + (lax.broadcasted_iota(jnp.int32, (2 * tq, 1), 0) & (tq - 1))
    carry_scr[...] = jnp.zeros(carry_scr.shape, jnp.float32)
    acc_scr[...] = jnp.zeros(acc_scr.shape, jnp.float32)

    def blocks_step(js, flags):
        rows = [pl.ds(pl.multiple_of(j * tk, tk), tk) for j in js]
        masks = []
        for j, masked in zip(js, flags):
            kpos = j * tk + lax.broadcasted_iota(jnp.int32, (1, tk), 1)
            masks.append(((kpos < qpos) & (kpos >= PAD)) if masked else None)
        cols = [slice(blk * LANES, (blk + 1) * LANES) for blk in range(n_blocks)]
        zs = [[_dot_nt(qs[blk], k_ref[r, cols[blk]]) for blk in range(n_blocks)] for r in rows]
        cums, ksums = [], []
        for b in range(len(js)):
            cums.append([])
            ksums.append([])
            for blk in range(n_blocks):
                z = zs[b][blk]
                keep = -(jnp.maximum(z, 0.0) + jnp.log2(1.0 + jnp.exp2(-jnp.abs(z))))
                if masks[b] is not None:
                    keep = jnp.where(masks[b], keep, 0.0)
                hi = keep.astype(bf)
                lw = (keep - hi.astype(jnp.float32)).astype(bf)
                cums[b].append(_bdot(jnp.concatenate([hi, lw], axis=1), tri2))
                ksums[b].append(jnp.sum(keep, axis=-1, keepdims=True))
        worst = None
        for blk in range(n_blocks):
            rs = slice(2 * blk * tq, 2 * (blk + 1) * tq)
            carry = carry_scr[rs, :]
            pv = None
            for b in range(len(js)):
                a = jnp.exp2(zs[b][blk] + (carry + cums[b][blk]))
                if masks[b] is not None:
                    a = jnp.where(masks[b], a, 0.0)
                d = _bdot(a.astype(bf), v_ref[rows[b], cols[blk]])
                pv = d if pv is None else pv + d
                carry = carry + ksums[b][blk]
            carry_scr[rs, :] = carry
            acc_scr[rs, :] += pv
            worst = carry if worst is None else jnp.maximum(worst, carry)
        return jnp.max(worst)

    def cond(st):
        j, worst = st
        return (j >= 0) & (worst > -STICK_SKIP_LOG2)

    def body(st):
        j, _ = st
        worst = lax.cond((j == i) | (j == 0), lambda: blocks_step([j], [True]), lambda: blocks_step([j], [False]))
        return j - 1, worst

    head = STICK_HEAD_BLOCKS
    start = lax.cond(i >= head,
                     lambda: (i - head, blocks_step([i - b for b in range(head)], [True] + [False] * (head - 1))),
                     lambda: (i, jnp.float32(0.0)))
    lax.while_loop(cond, body, start)
    for blk in range(n_blocks):
        r0 = 2 * blk * tq
        o_ref[:, blk * LANES:(blk + 1) * LANES] = jnp.where(
            lo, acc_scr[r0:r0 + tq, :], acc_scr[r0 + tq:r0 + 2 * tq, :]).astype(bf)


def _post_kernel(x_ref, g1_ref, wg_ref, gb_ref, oa_ref, ob_ref, oc_ref, wb_ref, wo_ref,
                 g2_ref, w1_ref, w2_ref, y_ref, *, n_chunks):
    bf = jnp.bfloat16
    x = x_ref[...]
    h = _rms(x, g1_ref[...], D_MODEL).astype(bf)
    outs = (oa_ref, ob_ref, oc_ref)
    merged = None
    for br in range(N_BRANCHES):
        cs = slice(br * D_MODEL, (br + 1) * D_MODEL)
        logits = _bdot(h, wg_ref[:, cs]) + gb_ref[:, cs]
        gate = 1.0 / (1.0 + jnp.exp(-logits))
        rs = slice(br * MLA_OUT, (br + 1) * MLA_OUT)
        term = gate * _bdot(outs[br][...], wb_ref[rs, :])
        merged = term if merged is None else merged + term
    x = x + _bdot(merged.astype(bf), wo_ref[...])

    h2 = _rms(x, g2_ref[...], D_MODEL).astype(bf)
    ck = D_FF // n_chunks
    acc = x
    for c in range(n_chunks):
        cs = slice(c * ck, (c + 1) * ck)
        f = jnp.maximum(_bdot(h2, w1_ref[:, cs]), 0.0)
        acc = acc + _bdot((f * f).astype(bf), w2_ref[cs, :])
    y_ref[...] = acc


def _rope_tables(p_len, d, blocks):
    half = d // 2
    pos = jnp.maximum(jnp.arange(p_len) - PAD, 0)
    inv_freq = jnp.exp(-math.log(ROPE_THETA) * (2.0 * jnp.arange(half, dtype=jnp.float32) / d))
    ang = pos.astype(jnp.float32)[:, None] * inv_freq[None, :]
    cos, sin = jnp.cos(ang), jnp.sin(ang)
    zero = jnp.zeros_like(sin)
    cos_p, sina_p, sinb_p, at = [], [], [], 0
    for off in blocks:
        gap = off - at
        cos_p += [jnp.ones((p_len, gap), jnp.float32), cos, cos]
        sina_p += [jnp.zeros((p_len, gap), jnp.float32), -sin, zero]
        sinb_p += [jnp.zeros((p_len, gap), jnp.float32), zero, sin]
        at = off + d
    cos_p.append(jnp.ones((p_len, LANES - at), jnp.float32))
    sina_p.append(jnp.zeros((p_len, LANES - at), jnp.float32))
    sinb_p.append(jnp.zeros((p_len, LANES - at), jnp.float32))
    return tuple(jnp.concatenate(parts, axis=1) for parts in (cos_p, sina_p, sinb_p))


def _swap_halves(t):
    half = t.shape[-1] // 2
    return jnp.concatenate([t[..., half:], t[..., :half]], axis=-1)


def _row(v):
    return v.reshape(1, -1).astype(jnp.float32)


def _full(shape):
    return pl.BlockSpec(shape, lambda *_: (0,) * len(shape), pipeline_mode=pl.Buffered(1))


def _layer_weights(layer, w_in, mla_w_uq, mla_w_ukv, w_branch, w_out, w_ff1, w_ff2):
    bf = jnp.bfloat16
    w = w_in[layer]
    offs = [0]
    for s in IN_SIZES:
        offs.append(offs[-1] + s)
    d = w.shape[0]
    w_kr = w[:, offs[2]:offs[3]]
    w_lat = jnp.concatenate([
        w[:, offs[0]:offs[2]],
        jnp.zeros((d, MLA_NOPE), w.dtype), w_kr,
        jnp.zeros((d, LANES - MLA_QK), w.dtype),
        jnp.zeros((d, MLA_NOPE), w.dtype), _swap_halves(w_kr),
        jnp.zeros((d, LANES - MLA_QK), w.dtype)], axis=1).astype(bf)
    w_sb = w[:, offs[3]:offs[6]].astype(bf)
    w_df = w[:, offs[6]:offs[9]].astype(bf)
    w_gate = w[:, offs[9]:offs[10]].astype(bf)
    wuq = mla_w_uq[layer].reshape(MLA_Q_LORA, MLA_HEADS, MLA_QK)
    wuqp = jnp.concatenate([wuq[..., :MLA_NOPE], _swap_halves(wuq[..., MLA_NOPE:])], axis=-1)
    wuq = jnp.pad(wuq, ((0, 0), (0, 0), (0, LANES - MLA_QK))).reshape(MLA_Q_LORA, MLA_HEADS * LANES).astype(bf)
    wuqp = jnp.pad(wuqp, ((0, 0), (0, 0), (0, LANES - MLA_QK))).reshape(MLA_Q_LORA, MLA_HEADS * LANES).astype(bf)
    wukv = mla_w_ukv[layer].reshape(MLA_KV_LORA, MLA_HEADS, MLA_NOPE + MLA_V)
    wk = jnp.pad(wukv[:, :, :MLA_NOPE], ((0, 0), (0, 0), (0, LANES - MLA_NOPE)))
    wk = wk.reshape(MLA_KV_LORA, MLA_HEADS * LANES).astype(bf)
    wv = wukv[:, :, MLA_NOPE:].reshape(MLA_KV_LORA, MLA_OUT).astype(bf)
    return dict(w_lat=w_lat, w_sb=w_sb, w_df=w_df, w_gate=w_gate, wuq=wuq, wuqp=wuqp, wk=wk, wv=wv,
                wb=w_branch[layer].astype(bf), wo=w_out[layer].astype(bf),
                w1=w_ff1[layer].astype(bf), w2=w_ff2[layer].astype(bf))


def kernel(x, meta_tokens, ln1_g, w_in, mla_cq_norm_g, mla_ckv_norm_g, mla_w_uq, mla_w_ukv,
           mla_q_norm_g, mla_k_norm_g, diff_q_norm_g, diff_k_norm_g, diff_lambda,
           diff_out_norm_g, gate_b, w_branch, w_out, ln2_g, w_ff1, w_ff2):
    b, seq, d = x.shape
    assert d == D_MODEL
    bf = jnp.bfloat16
    pad = jnp.zeros((b, PAD, d), x.dtype)
    meta = jnp.broadcast_to(meta_tokens.astype(x.dtype)[None], (b, N_META, d))
    h_res = jnp.concatenate([pad, meta, x], axis=1)
    p_len = h_res.shape[1]
    t = _tiles(p_len)
    tm, tqs, tqb = t['tm'], t['t_soft'], t['t_sb']
    n_tm = p_len // tm

    tab_a = _rope_tables(p_len, MLA_ROPE, [MLA_NOPE])
    tab_c = _rope_tables(p_len, DIFF_HEAD_DIM, [0, HALF_LANES])
    tri = (jnp.arange(tqb)[:, None] >= jnp.arange(tqb)[None, :]).astype(bf)
    tri2 = jnp.concatenate([tri, tri], axis=0)

    cparams2 = pltpu.CompilerParams(dimension_semantics=("parallel", "parallel"),
                                    vmem_limit_bytes=VMEM_LIMIT_BYTES)
    cparams2s = pltpu.CompilerParams(dimension_semantics=("parallel", "arbitrary"),
                                     vmem_limit_bytes=VMEM_LIMIT_BYTES)

    def tok(width):
        return pl.BlockSpec((None, tm, width), lambda bi, ti: (bi, ti, 0))

    def tab():
        return pl.BlockSpec((tm, LANES), lambda bi, ti: (ti, 0))

    def act(width):
        return jax.ShapeDtypeStruct((b, p_len, width), bf)

    def soft_scratch(v_maps):
        return [pltpu.VMEM((2 * tqs, LANES), jnp.float32),
                pltpu.VMEM((2 * tqs, 2 * LANES), jnp.float32),
                pltpu.VMEM((2 * tqs, MXU_DIM), jnp.float32), pltpu.VMEM((2 * tqs, MXU_DIM), jnp.float32),
                pltpu.VMEM((2 * tqs, LANES), jnp.float32), pltpu.VMEM((2 * tqs, LANES), jnp.float32),
                pltpu.VMEM((v_maps * p_len, 2 * LANES), bf)]

    def seq_block(width):
        return pl.BlockSpec((None, p_len, width), lambda bi, ci: (bi, 0, ci))

    for layer in range(DEPTH):
        wts = _layer_weights(layer, w_in, mla_w_uq, mla_w_ukv, w_branch, w_out, w_ff1, w_ff2)
        g1 = _row(ln1_g[layer])
        gdq = _row(jnp.tile(diff_q_norm_g[layer], 2))
        gdk = _row(jnp.tile(diff_k_norm_g[layer], 2))

        def mla_tables(gain, scale):
            g = gain.astype(jnp.float32)
            g_own = _row(jnp.pad(g, (0, LANES - MLA_QK)))
            g_partner = _row(jnp.pad(jnp.concatenate([g[:MLA_NOPE], _swap_halves(g[MLA_NOPE:])]),
                                     (0, LANES - MLA_QK)))
            return tab_a[0] * (g_own * scale), (tab_a[1] + tab_a[2]) * (g_partner * scale)

        c1q, c2q = mla_tables(mla_q_norm_g[layer], (MLA_QK ** -0.5) * LOG2E)
        c1k, c2k = mla_tables(mla_k_norm_g[layer], 1.0)

        pre_in = [h_res, g1, wts['w_lat'], _row(mla_cq_norm_g[layer]), _row(mla_ckv_norm_g[layer]),
                  wts['wuq'], wts['wuqp'], wts['wk'], wts['wv'], c1q, c2q, c1k, c2k,
                  wts['w_sb'], wts['w_df'], gdq, gdk, *tab_c]
        pre_specs = [tok(d)] + [_full(a.shape) for a in pre_in[1:9]] + [tab()] * 4 \
            + [_full(a.shape) for a in pre_in[13:17]] + [tab()] * 3
        widths = (MLA_HEADS * LANES, MLA_HEADS * LANES, MLA_OUT, SB_OUT, SB_OUT, SB_OUT,
                  DIFF_QK, DIFF_QK, DIFF_OUT)
        qa, ka, va, qb, kb, vb, qc, kc, vc = pl.pallas_call(
            _pre_kernel, grid=(b, n_tm), in_specs=pre_specs,
            out_specs=[tok(wd) for wd in widths], out_shape=[act(wd) for wd in widths],
            compiler_params=cparams2, name="pre_tokens")(*pre_in)

        out_a = pl.pallas_call(
            functools.partial(_softmax2_kernel, mode='mla', p_len=p_len, tile=tqs),
            grid=(b, MLA_HEADS // 2),
            in_specs=[seq_block(2 * LANES), seq_block(2 * LANES), seq_block(LANES)],
            out_specs=seq_block(LANES), scratch_shapes=soft_scratch(2),
            out_shape=act(MLA_OUT), compiler_params=cparams2, name="attn_mla")(qa, ka, va)

        n_qb = p_len // tqb
        out_b = pl.pallas_call(
            functools.partial(_stick_kernel, tq=tqb, tk=tqb),
            grid=(b, n_qb),
            in_specs=[pl.BlockSpec((None, tqb, SB_OUT), lambda bi, qi: (bi, qi, 0)),
                      pl.BlockSpec((None, p_len, SB_OUT), lambda bi, qi: (bi, 0, 0)),
                      pl.BlockSpec((None, p_len, SB_OUT), lambda bi, qi: (bi, 0, 0)),
                      _full(tri2.shape)],
            out_specs=pl.BlockSpec((None, tqb, SB_OUT), lambda bi, qi: (bi, qi, 0)),
            scratch_shapes=[pltpu.VMEM((SB_HEADS * tqb, LANES), jnp.float32)] * 2,
            out_shape=act(SB_OUT), compiler_params=cparams2s, name="attn_stick")(qb, kb, vb, tri2)

        lam_init = 0.8 - 0.6 * math.exp(-0.3 * layer)
        g_c = _row(diff_out_norm_g[layer])
        out_c = pl.pallas_call(
            functools.partial(_softmax2_kernel, mode='diff', p_len=p_len, tile=tqs, lam_init=lam_init),
            grid=(b, DIFF_HEADS),
            in_specs=[seq_block(LANES), seq_block(LANES), seq_block(LANES),
                      _full((4, DIFF_HEAD_DIM)), _full((1, LANES))],
            out_specs=seq_block(LANES), scratch_shapes=soft_scratch(1),
            out_shape=act(DIFF_OUT), compiler_params=cparams2, name="attn_diff")(
                qc, kc, vc, diff_lambda[layer].astype(jnp.float32), g_c)

        g2 = _row(ln2_g[layer])
        post_in = [h_res, g1, wts['w_gate'], _row(gate_b[layer]), out_a, out_b, out_c, wts['wb'], wts['wo'],
                   g2, wts['w1'], wts['w2']]
        post_specs = [tok(d)] + [_full(a.shape) for a in post_in[1:4]] \
            + [tok(MLA_OUT), tok(SB_OUT), tok(DIFF_OUT)] + [_full(a.shape) for a in post_in[7:]]
        h_res = pl.pallas_call(
            functools.partial(_post_kernel, n_chunks=4), grid=(b, n_tm), in_specs=post_specs, out_specs=tok(d),
            out_shape=jax.ShapeDtypeStruct((b, p_len, d), jnp.float32),
            compiler_params=cparams2, name="merge_ffn")(*post_in)

    return h_res[:, PAD + N_META:]
```

```python
import functools
import math

import jax
import jax.numpy as jnp
from jax import lax
from jax.experimental import pallas as pl
from jax.experimental.pallas import tpu as pltpu

D_MODEL = 1024
DEPTH = 2
N_META = 16
BLOCK = 128
PAD = BLOCK - N_META
ROPE_THETA = 10000.0
NORM_EPS = 1e-6
MASK_VALUE = -1e30

MLA_HEADS = 8
MLA_Q_LORA = 256
MLA_KV_LORA = 128
MLA_NOPE = 64
MLA_ROPE = 32
MLA_V = 64
MLA_QK = MLA_NOPE + MLA_ROPE
SB_HEADS = 8
SB_HEAD_DIM = 64
DIFF_HEADS = 4
DIFF_HEAD_DIM = 64
DIFF_V_DIM = 2 * DIFF_HEAD_DIM
N_BRANCHES = 3
MLA_OUT = MLA_HEADS * MLA_V
SB_OUT = SB_HEADS * SB_HEAD_DIM
DIFF_QK = DIFF_HEADS * 2 * DIFF_HEAD_DIM
DIFF_OUT = DIFF_HEADS * DIFF_V_DIM
D_FF = 4 * D_MODEL
IN_SIZES = (MLA_Q_LORA, MLA_KV_LORA, MLA_ROPE, SB_OUT, SB_OUT, SB_OUT,
            DIFF_QK, DIFF_QK, DIFF_OUT, N_BRANCHES * D_MODEL)

LANES = 128
HALF_LANES = LANES // 2
MXU_DIM = 256
CHAIN = 128

LOG2E = math.log2(math.e)
VMEM_LIMIT_BYTES = 56 * 1024 * 1024
STICK_SKIP_LOG2 = 110.0 * LOG2E
STICK_HEAD_BLOCKS = 3

_NT = (((1,), (1,)), ((), ()))


def _tiles(p_len):
    assert p_len % 528 == 0 and (p_len - BLOCK) % (2 * MXU_DIM) == 0 and p_len % (11 * BLOCK) == 0, p_len
    return dict(tm=528, t_soft=2 * MXU_DIM, t_sb=BLOCK, sb_tiles_per_step=11)


def _rms(x, g, n):
    ss = jnp.sum(x * x, axis=-1, keepdims=True) * (1.0 / n)
    return (x * lax.rsqrt(ss + NORM_EPS)) * g


def _rope_lanes(x, cos, sina, sinb, half):
    return x * cos + pltpu.roll(x, LANES - half, 1) * sina + pltpu.roll(x, half, 1) * sinb


def _bdot(a, b):
    return jnp.dot(a, b, preferred_element_type=jnp.float32)


def _dot_nt(a, b):
    return lax.dot_general(a, b, _NT, preferred_element_type=jnp.float32)


def _pre_kernel(x_ref, g1_ref, wlat_ref, gcq_ref, gckv_ref, wuq_ref, wuqp_ref, wk_ref, wv_ref,
                c1q_ref, c2q_ref, c1k_ref, c2k_ref,
                wsb_ref, wdf_ref, gdq_ref, gdk_ref, cosc_ref, sinac_ref, sinbc_ref,
                qa_ref, ka_ref, va_ref, qb_ref, kb_ref, vb_ref, qc_ref, kc_ref, vc_ref):
    bf = jnp.bfloat16
    x = x_ref[...]
    h = _rms(x, g1_ref[...], D_MODEL).astype(bf)

    df = _bdot(h, wdf_ref[...])
    cosc, sinac, sinbc = cosc_ref[...], sinac_ref[...], sinbc_ref[...]
    lane = lax.broadcasted_iota(jnp.int32, (1, LANES), 1)
    lo = lane < HALF_LANES
    c_scale = (DIFF_HEAD_DIM ** -0.5) * LOG2E

    def half_norm(t, g):
        sq = t * t
        ss_lo = jnp.sum(jnp.where(lo, sq, 0.0), axis=-1, keepdims=True)
        ss_hi = jnp.sum(jnp.where(lo, 0.0, sq), axis=-1, keepdims=True)
        r = jnp.where(lo, lax.rsqrt(ss_lo * (1.0 / DIFF_HEAD_DIM) + NORM_EPS),
                      lax.rsqrt(ss_hi * (1.0 / DIFF_HEAD_DIM) + NORM_EPS))
        return (t * r) * g

    for hh in range(DIFF_HEADS):
        sl = slice(hh * LANES, (hh + 1) * LANES)
        qh = _rope_lanes(half_norm(df[:, sl], gdq_ref[...]), cosc, sinac, sinbc, DIFF_HEAD_DIM // 2)
        qc_ref[:, sl] = (qh * c_scale).astype(bf)
        sk = slice(DIFF_QK + hh * LANES, DIFF_QK + (hh + 1) * LANES)
        kh = _rope_lanes(half_norm(df[:, sk], gdk_ref[...]), cosc, sinac, sinbc, DIFF_HEAD_DIM // 2)
        kc_ref[:, sl] = kh.astype(bf)
    vc_ref[...] = df[:, 2 * DIFF_QK:].astype(bf)

    lat = _bdot(h, wlat_ref[...])
    cq = _rms(lat[:, :MLA_Q_LORA], gcq_ref[...], MLA_Q_LORA).astype(bf)
    ckv = _rms(lat[:, MLA_Q_LORA:MLA_Q_LORA + MLA_KV_LORA], gckv_ref[...], MLA_KV_LORA).astype(bf)
    kr_off = MLA_Q_LORA + MLA_KV_LORA
    kr_block = lat[:, kr_off:kr_off + LANES]
    krp_c2 = lat[:, kr_off + LANES:] * c2k_ref[...]
    qf = _bdot(cq, wuq_ref[...])
    qpf = _bdot(cq, wuqp_ref[...])
    kf = _bdot(ckv, wk_ref[...])
    va_ref[...] = _bdot(ckv, wv_ref[...]).astype(bf)
    c1q, c2q, c1k = c1q_ref[...], c2q_ref[...], c1k_ref[...]

    def inv_rms(t):
        return lax.rsqrt(jnp.sum(t * t, axis=-1, keepdims=True) * (1.0 / MLA_QK) + NORM_EPS)

    for hh in range(MLA_HEADS):
        sl = slice(hh * LANES, (hh + 1) * LANES)
        qx = qf[:, sl]
        qa_ref[:, sl] = ((qx * c1q + qpf[:, sl] * c2q) * inv_rms(qx)).astype(bf)
        kx = kf[:, sl] + kr_block
        ka_ref[:, sl] = ((kx * c1k + krp_c2) * inv_rms(kx)).astype(bf)

    sb = _bdot(h, wsb_ref[...])
    qb_ref[...] = (sb[:, :SB_OUT] * ((SB_HEAD_DIM ** -0.5) * LOG2E)).astype(bf)
    kb_ref[...] = sb[:, SB_OUT:2 * SB_OUT].astype(bf)
    vb_ref[...] = sb[:, 2 * SB_OUT:].astype(bf)


def _softmax2_kernel(*refs, mode, p_len, tile, lam_init=None):
    if mode == 'mla':
        q_ref, k_ref, v_ref, o_ref, m_scr, acc_scr, s0_scr, s1_scr, mx0_scr, mx1_scr, vx_scr = refs
    else:
        q_ref, k_ref, v_ref, lam_ref, g_ref, o_ref, m_scr, acc_scr, s0_scr, s1_scr, mx0_scr, mx1_scr, vx_scr = refs
    bf = jnp.bfloat16
    kb = MXU_DIM
    assert tile % (2 * kb) == 0, (tile, kb)
    kpt = tile // kb
    n_tiles = (p_len - BLOCK) // tile
    n_kb = (p_len - BLOCK) // kb
    n_half = tile // CHAIN
    n_chain = 2 * n_half
    lane = lax.broadcasted_iota(jnp.int32, (1, LANES), 1)
    lo = lane < HALF_LANES

    if mode == 'diff':
        lp = lam_ref[...]
        lam = (jnp.exp(jnp.sum(lp[0:1] * lp[1:2], axis=-1, keepdims=True))
               - jnp.exp(jnp.sum(lp[2:3] * lp[3:4], axis=-1, keepdims=True)) + lam_init)

    def load_q(r0, n):
        rows = pl.ds(r0, n)
        if mode == 'mla':
            return q_ref[rows, :LANES], q_ref[rows, LANES:]
        q = q_ref[rows, :]
        return q, q

    def k_block(r0, n, m):
        rows = pl.ds(r0, n)
        if mode == 'mla':
            return k_ref[rows, m * LANES:(m + 1) * LANES]
        k = k_ref[rows, :]
        return jnp.where(lo if m == 0 else jnp.logical_not(lo), k, jnp.zeros_like(k))

    def combine(o0, o1):
        if mode == 'mla':
            return jnp.where(lo, o0, o1)
        return _rms(o0 - lam * o1, g_ref[...], DIFF_V_DIM) * (1.0 - lam_init)

    row = lax.broadcasted_iota(jnp.int32, (CHAIN, 1), 0)
    col0 = lax.broadcasted_iota(jnp.int32, (1, BLOCK), 1)
    meta_ok = col0 >= PAD

    q_meta = load_q(0, BLOCK)
    v_meta = v_ref[pl.ds(0, BLOCK), :]
    o_meta = []
    for m in range(2):
        s = jnp.where((col0 <= row) & meta_ok, _dot_nt(q_meta[m], k_block(0, BLOCK, m)), MASK_VALUE)
        p = jnp.exp2(s - jnp.max(s, axis=-1, keepdims=True))
        o_meta.append(_bdot(p.astype(bf), v_meta) / jnp.sum(p, axis=-1, keepdims=True))
    o_ref[pl.ds(0, BLOCK), :] = combine(*o_meta).astype(bf)

    col = lax.broadcasted_iota(jnp.int32, (1, kb), 1)
    diag = [[col + d * kb <= row + (c % n_half) * CHAIN for c in range(n_chain)] for d in range(kpt)]

    def key_row(j):
        return pl.multiple_of(BLOCK + j * kb, BLOCK)

    def qk_into(bufs, qs, r0, masks, h_from=0):
        s_scr, mx_scr = bufs
        for m in range(2):
            s = _dot_nt(qs[m][h_from * CHAIN:], k_block(r0, kb, m))
            for h in range(h_from, n_half):
                c = m * n_half + h
                rs = slice(c * CHAIN, (c + 1) * CHAIN)
                s_c = s[(h - h_from) * CHAIN:(h - h_from + 1) * CHAIN]
                if masks is not None:
                    s_c = jnp.where(masks[c], s_c, MASK_VALUE)
                s_scr[rs, :] = s_c
                mx_scr[rs, :] = jnp.broadcast_to(jnp.max(s_c, axis=-1, keepdims=True), (CHAIN, LANES))

    v_maps = 2 if mode == 'mla' else 1

    def v_ext(vblk, m):
        if mode == 'diff':
            return jnp.concatenate([vblk, jnp.ones_like(vblk)], axis=1)
        mine = lo if m == 0 else jnp.logical_not(lo)
        ones = jnp.broadcast_to(jnp.where(mine, 1.0, 0.0).astype(bf), vblk.shape)
        return jnp.concatenate([jnp.where(mine, vblk, jnp.zeros_like(vblk)), ones], axis=1)

    vx_meta_base = n_kb * v_maps * kb
    for m in range(v_maps):
        vx_scr[pl.ds(vx_meta_base + m * BLOCK, BLOCK), :] = v_ext(v_meta, m)

    def build_vx(j, c_):
        vblk = v_ref[pl.ds(key_row(j), kb), :]
        base = pl.multiple_of(j * v_maps * kb, kb)
        for m in range(v_maps):
            vx_scr[pl.ds(base + m * kb, kb), :] = v_ext(vblk, m)
        return c_

    lax.fori_loop(0, n_kb, build_vx, 0)

    def update(get_s, get_max, vx_base, n, h_from=0):
        rhs = vx_scr[pl.ds(vx_base, v_maps * n), :]
        for h in range(h_from, n_half):
            parts, alphas = [], []
            for m in range(2):
                c = m * n_half + h
                rs = slice(c * CHAIN, (c + 1) * CHAIN)
                s = get_s(c)
                m_run = m_scr[rs, :]
                m_new = jnp.maximum(m_run, get_max(c))
                alpha = jnp.exp2(m_run - m_new)
                m_scr[rs, :] = m_new
                ps = [jnp.exp2(s[:, k * LANES:(k + 1) * LANES] - m_new).astype(bf) for k in range(n // LANES)]
                if mode == 'diff':
                    p = ps[0] if len(ps) == 1 else jnp.concatenate(ps, axis=1)
                    acc_scr[rs, :] = jnp.concatenate([alpha, alpha], axis=1) * acc_scr[rs, :] + _bdot(p, rhs)
                parts += ps
                alphas.append(alpha)
            if mode == 'mla':
                both = jnp.where(lo, alphas[0], alphas[1])
                ah = slice(h * CHAIN, (h + 1) * CHAIN)
                acc_scr[ah, :] = (jnp.concatenate([both, both], axis=1) * acc_scr[ah, :]
                                  + _bdot(jnp.concatenate(parts, axis=1), rhs))

    def update_from(bufs, vx_base, h_from=0):
        s_scr, mx_scr = bufs
        update(lambda c: s_scr[c * CHAIN:(c + 1) * CHAIN, :], lambda c: mx_scr[c * CHAIN:(c + 1) * CHAIN, :],
               vx_base, kb, h_from)

    def vx_row(j):
        return pl.multiple_of(j * v_maps * kb, kb)

    buf0, buf1 = (s0_scr, mx0_scr), (s1_scr, mx1_scr)

    def q_tile(i, carry):
        r0 = pl.multiple_of(BLOCK + i * tile, BLOCK)
        qs = load_q(r0, tile)
        m_scr[...] = jnp.full(m_scr.shape, MASK_VALUE, jnp.float32)
        acc_scr[...] = jnp.zeros(acc_scr.shape, jnp.float32)
        first = i == 0
        qk_into(buf0, qs, key_row(0), [col <= row + (c % n_half) * CHAIN + i * tile for c in range(n_chain)])

        s_meta = [jnp.where(meta_ok, _dot_nt(qs[m], k_block(0, BLOCK, m)), MASK_VALUE) for m in range(2)]

        def meta_s(c):
            return s_meta[c // n_half][(c % n_half) * CHAIN:(c % n_half + 1) * CHAIN]

        update(meta_s, lambda c: jnp.max(meta_s(c), axis=-1, keepdims=True), vx_meta_base, BLOCK)

        def pair_at(j):
            qk_into(buf1, qs, key_row(j + 1), None)
            update_from(buf0, vx_row(j))
            qk_into(buf0, qs, key_row(j + 2), None)
            update_from(buf1, vx_row(j + 1))

        def quad(t, c_):
            pair_at(4 * t)
            pair_at(4 * t + 2)
            return c_

        jd = kpt * i
        n_pairs = jnp.maximum(lax.shift_right_logical(jd, 1) - 1, 0)
        lax.fori_loop(0, lax.shift_right_logical(n_pairs, 1), quad, 0)

        @pl.when((n_pairs & 1) == 1)
        def _():
            pair_at(2 * (n_pairs - 1))

        @pl.when(jnp.logical_not(first))
        def _():
            qk_into(buf1, qs, key_row(jd - 1), None)
            update_from(buf0, vx_row(jd - 2))
            qk_into(buf0, qs, key_row(jd), diag[0])
            update_from(buf1, vx_row(jd - 1))

        bufs = (buf0, buf1)
        for dg in range(kpt):
            if dg + 1 < kpt:
                qk_into(bufs[(dg + 1) % 2], qs, key_row(jd + dg + 1), diag[dg + 1], (dg + 1) * kb // CHAIN)
            update_from(bufs[dg % 2], vx_row(jd + dg), dg * kb // CHAIN)

        for h in range(n_half):
            a = acc_scr[h * CHAIN:(h + 1) * CHAIN, :]
            if mode == 'mla':
                res = a[:, :LANES] / a[:, LANES:]
            else:
                a1 = acc_scr[(n_half + h) * CHAIN:(n_half + h + 1) * CHAIN, :]
                res = combine(a[:, :LANES] / a[:, LANES:], a1[:, :LANES] / a1[:, LANES:])
            o_ref[pl.ds(r0 + h * CHAIN, CHAIN), :] = res.astype(bf)
        return carry

    lax.fori_loop(0, n_tiles, q_tile, 0)


def _stick_kernel(q_ref, k_ref, v_ref, tri_ref, o_ref, carry_scr, acc_scr, *, tq, tk, tiles_per_step):
    bf = jnp.bfloat16
    n_blocks = q_ref.shape[-1] // LANES
    lane = lax.broadcasted_iota(jnp.int32, (1, LANES), 1)
    lo = lane < HALF_LANES
    tri2 = tri_ref[...]

    def sub_tile(sub, c_):
        i = pl.program_id(1) * tiles_per_step + sub
        q_rows = pl.ds(pl.multiple_of(sub * tq, tq), tq)
        qs = []
        for blk in range(n_blocks):
            q = q_ref[q_rows, blk * LANES:(blk + 1) * LANES]
            zero = jnp.zeros_like(q)
            qs.append(jnp.concatenate([jnp.where(lo, q, zero), jnp.where(lo, zero, q)], axis=0))
        qpos = i * tq + (lax.broadcasted_iota(jnp.int32, (2 * tq, 1), 0) & (tq - 1))
        carry_scr[...] = jnp.zeros(carry_scr.shape, jnp.float32)
        acc_scr[...] = jnp.zeros(acc_scr.shape, jnp.float32)

        def blocks_step(js, flags):
            rows = [pl.ds(pl.multiple_of(j * tk, tk), tk) for j in js]
            masks = []
            for j, masked in zip(js, flags):
                kpos = j * tk + lax.broadcasted_iota(jnp.int32, (1, tk), 1)
                masks.append(((kpos < qpos) & (kpos >= PAD)) if masked else None)
            cols = [slice(blk * LANES, (blk + 1) * LANES) for blk in range(n_blocks)]
            zs = [[_dot_nt(qs[blk], k_ref[r, cols[blk]]) for blk in range(n_blocks)] for r in rows]
            cums, ksums = [], []
            for b in range(len(js)):
                cums.append([])
                ksums.append([])
                for blk in range(n_blocks):
                    z = zs[b][blk]
                    keep = -(jnp.maximum(z, 0.0) + jnp.log2(1.0 + jnp.exp2(-jnp.abs(z))))
                    if masks[b] is not None:
                        keep = jnp.where(masks[b], keep, 0.0)
                    hi = keep.astype(bf)
                    lw = (keep - hi.astype(jnp.float32)).astype(bf)
                    cums[b].append(_bdot(jnp.concatenate([hi, lw], axis=1), tri2))
                    ksums[b].append(jnp.sum(keep, axis=-1, keepdims=True))
            worst = None
            for blk in range(n_blocks):
                rs = slice(2 * blk * tq, 2 * (blk + 1) * tq)
                carry = carry_scr[rs, :]
                pv = None
                for b in range(len(js)):
                    a = jnp.exp2(zs[b][blk] + (carry + cums[b][blk]))
                    if masks[b] is not None:
                        a = jnp.where(masks[b], a, 0.0)
                    d = _bdot(a.astype(bf), v_ref[rows[b], cols[blk]])
                    pv = d if pv is None else pv + d
                    carry = carry + ksums[b][blk]
                carry_scr[rs, :] = carry
                acc_scr[rs, :] += pv
                worst = carry if worst is None else jnp.maximum(worst, carry)
            return jnp.max(worst)

        def cond(st):
            j, worst = st
            return (j >= 0) & (worst > -STICK_SKIP_LOG2)

        def body(st):
            j, _ = st
            worst = lax.cond((j == i) | (j == 0), lambda: blocks_step([j], [True]), lambda: blocks_step([j], [False]))
            return j - 1, worst

        head = STICK_HEAD_BLOCKS
        start = lax.cond(i >= head,
                         lambda: (i - head, blocks_step([i - b for b in range(head)], [True] + [False] * (head - 1))),
                         lambda: (i, jnp.float32(0.0)))
        lax.while_loop(cond, body, start)
        for blk in range(n_blocks):
            r0 = 2 * blk * tq
            o_ref[q_rows, blk * LANES:(blk + 1) * LANES] = jnp.where(
                lo, acc_scr[r0:r0 + tq, :], acc_scr[r0 + tq:r0 + 2 * tq, :]).astype(bf)
        return c_

    lax.fori_loop(0, tiles_per_step, sub_tile, 0)


def _post_kernel(x_ref, g1_ref, wg_ref, gb_ref, oa_ref, ob_ref, oc_ref, wb_ref, wo_ref,
                 g2_ref, w1_ref, w2_ref, y_ref, *, n_chunks):
    bf = jnp.bfloat16
    x = x_ref[...]
    h = _rms(x, g1_ref[...], D_MODEL).astype(bf)
    outs = (oa_ref, ob_ref, oc_ref)
    merged = None
    for br in range(N_BRANCHES):
        cs = slice(br * D_MODEL, (br + 1) * D_MODEL)
        logits = _bdot(h, wg_ref[:, cs]) + gb_ref[:, cs]
        gate = 1.0 / (1.0 + jnp.exp(-logits))
        rs = slice(br * MLA_OUT, (br + 1) * MLA_OUT)
        term = gate * _bdot(outs[br][...], wb_ref[rs, :])
        merged = term if merged is None else merged + term
    x = x + _bdot(merged.astype(bf), wo_ref[...])

    h2 = _rms(x, g2_ref[...], D_MODEL).astype(bf)
    ck = D_FF // n_chunks
    acc = x
    for c in range(n_chunks):
        cs = slice(c * ck, (c + 1) * ck)
        f = jnp.maximum(_bdot(h2, w1_ref[:, cs]), 0.0)
        acc = acc + _bdot((f * f).astype(bf), w2_ref[cs, :])
    y_ref[...] = acc


def _rope_tables(p_len, d, blocks):
    half = d // 2
    pos = jnp.maximum(jnp.arange(p_len) - PAD, 0)
    inv_freq = jnp.exp(-math.log(ROPE_THETA) * (2.0 * jnp.arange(half, dtype=jnp.float32) / d))
    ang = pos.astype(jnp.float32)[:, None] * inv_freq[None, :]
    cos, sin = jnp.cos(ang), jnp.sin(ang)
    zero = jnp.zeros_like(sin)
    cos_p, sina_p, sinb_p, at = [], [], [], 0
    for off in blocks:
        gap = off - at
        cos_p += [jnp.ones((p_len, gap), jnp.float32), cos, cos]
        sina_p += [jnp.zeros((p_len, gap), jnp.float32), -sin, zero]
        sinb_p += [jnp.zeros((p_len, gap), jnp.float32), zero, sin]
        at = off + d
    cos_p.append(jnp.ones((p_len, LANES - at), jnp.float32))
    sina_p.append(jnp.zeros((p_len, LANES - at), jnp.float32))
    sinb_p.append(jnp.zeros((p_len, LANES - at), jnp.float32))
    return tuple(jnp.concatenate(parts, axis=1) for parts in (cos_p, sina_p, sinb_p))


def _swap_halves(t):
    half = t.shape[-1] // 2
    return jnp.concatenate([t[..., half:], t[..., :half]], axis=-1)


def _row(v):
    return v.reshape(1, -1).astype(jnp.float32)


def _full(shape):
    return pl.BlockSpec(shape, lambda *_: (0,) * len(shape), pipeline_mode=pl.Buffered(1))


def _layer_weights(layer, w_in, mla_w_uq, mla_w_ukv, w_branch, w_out, w_ff1, w_ff2):
    bf = jnp.bfloat16
    w = w_in[layer]
    offs = [0]
    for s in IN_SIZES:
        offs.append(offs[-1] + s)
    d = w.shape[0]
    w_kr = w[:, offs[2]:offs[3]]
    w_lat = jnp.concatenate([
        w[:, offs[0]:offs[2]],
        jnp.zeros((d, MLA_NOPE), w.dtype), w_kr,
        jnp.zeros((d, LANES - MLA_QK), w.dtype),
        jnp.zeros((d, MLA_NOPE), w.dtype), _swap_halves(w_kr),
        jnp.zeros((d, LANES - MLA_QK), w.dtype)], axis=1).astype(bf)
    w_sb = w[:, offs[3]:offs[6]].astype(bf)
    w_df = w[:, offs[6]:offs[9]].astype(bf)
    w_gate = w[:, offs[9]:offs[10]].astype(bf)
    wuq = mla_w_uq[layer].reshape(MLA_Q_LORA, MLA_HEADS, MLA_QK)
    wuqp = jnp.concatenate([wuq[..., :MLA_NOPE], _swap_halves(wuq[..., MLA_NOPE:])], axis=-1)
    wuq = jnp.pad(wuq, ((0, 0), (0, 0), (0, LANES - MLA_QK))).reshape(MLA_Q_LORA, MLA_HEADS * LANES).astype(bf)
    wuqp = jnp.pad(wuqp, ((0, 0), (0, 0), (0, LANES - MLA_QK))).reshape(MLA_Q_LORA, MLA_HEADS * LANES).astype(bf)
    wukv = mla_w_ukv[layer].reshape(MLA_KV_LORA, MLA_HEADS, MLA_NOPE + MLA_V)
    wk = jnp.pad(wukv[:, :, :MLA_NOPE], ((0, 0), (0, 0), (0, LANES - MLA_NOPE)))
    wk = wk.reshape(MLA_KV_LORA, MLA_HEADS * LANES).astype(bf)
    wv = wukv[:, :, MLA_NOPE:].reshape(MLA_KV_LORA, MLA_OUT).astype(bf)
    return dict(w_lat=w_lat, w_sb=w_sb, w_df=w_df, w_gate=w_gate, wuq=wuq, wuqp=wuqp, wk=wk, wv=wv,
                wb=w_branch[layer].astype(bf), wo=w_out[layer].astype(bf),
                w1=w_ff1[layer].astype(bf), w2=w_ff2[layer].astype(bf))


def kernel(x, meta_tokens, ln1_g, w_in, mla_cq_norm_g, mla_ckv_norm_g, mla_w_uq, mla_w_ukv,
           mla_q_norm_g, mla_k_norm_g, diff_q_norm_g, diff_k_norm_g, diff_lambda,
           diff_out_norm_g, gate_b, w_branch, w_out, ln2_g, w_ff1, w_ff2):
    b, seq, d = x.shape
    assert d == D_MODEL
    bf = jnp.bfloat16
    pad = jnp.zeros((b, PAD, d), x.dtype)
    meta = jnp.broadcast_to(meta_tokens.astype(x.dtype)[None], (b, N_META, d))
    h_res = jnp.concatenate([pad, meta, x], axis=1)
    p_len = h_res.shape[1]
    t = _tiles(p_len)
    tm, tqs, tqb = t['tm'], t['t_soft'], t['t_sb']
    n_tm = p_len // tm

    tab_a = _rope_tables(p_len, MLA_ROPE, [MLA_NOPE])
    tab_c = _rope_tables(p_len, DIFF_HEAD_DIM, [0, HALF_LANES])
    tri = (jnp.arange(tqb)[:, None] >= jnp.arange(tqb)[None, :]).astype(bf)
    tri2 = jnp.concatenate([tri, tri], axis=0)

    cparams2 = pltpu.CompilerParams(dimension_semantics=("parallel", "parallel"),
                                    vmem_limit_bytes=VMEM_LIMIT_BYTES)
    cparams2s = pltpu.CompilerParams(dimension_semantics=("parallel", "arbitrary"),
                                     vmem_limit_bytes=VMEM_LIMIT_BYTES)

    def tok(width):
        return pl.BlockSpec((None, tm, width), lambda bi, ti: (bi, ti, 0))

    def tab():
        return pl.BlockSpec((tm, LANES), lambda bi, ti: (ti, 0))

    def act(width):
        return jax.ShapeDtypeStruct((b, p_len, width), bf)

    def soft_scratch(v_maps):
        return [pltpu.VMEM((2 * tqs, LANES), jnp.float32),
                pltpu.VMEM((2 * tqs, 2 * LANES), jnp.float32),
                pltpu.VMEM((2 * tqs, MXU_DIM), jnp.float32), pltpu.VMEM((2 * tqs, MXU_DIM), jnp.float32),
                pltpu.VMEM((2 * tqs, LANES), jnp.float32), pltpu.VMEM((2 * tqs, LANES), jnp.float32),
                pltpu.VMEM((v_maps * p_len, 2 * LANES), bf)]

    def seq_block(width):
        return pl.BlockSpec((None, p_len, width), lambda bi, ci: (bi, 0, ci))

    for layer in range(DEPTH):
        wts = _layer_weights(layer, w_in, mla_w_uq, mla_w_ukv, w_branch, w_out, w_ff1, w_ff2)
        g1 = _row(ln1_g[layer])
        gdq = _row(jnp.tile(diff_q_norm_g[layer], 2))
        gdk = _row(jnp.tile(diff_k_norm_g[layer], 2))

        def mla_tables(gain, scale):
            g = gain.astype(jnp.float32)
            g_own = _row(jnp.pad(g, (0, LANES - MLA_QK)))
            g_partner = _row(jnp.pad(jnp.concatenate([g[:MLA_NOPE], _swap_halves(g[MLA_NOPE:])]),
                                     (0, LANES - MLA_QK)))
            return tab_a[0] * (g_own * scale), (tab_a[1] + tab_a[2]) * (g_partner * scale)

        c1q, c2q = mla_tables(mla_q_norm_g[layer], (MLA_QK ** -0.5) * LOG2E)
        c1k, c2k = mla_tables(mla_k_norm_g[layer], 1.0)

        pre_in = [h_res, g1, wts['w_lat'], _row(mla_cq_norm_g[layer]), _row(mla_ckv_norm_g[layer]),
                  wts['wuq'], wts['wuqp'], wts['wk'], wts['wv'], c1q, c2q, c1k, c2k,
                  wts['w_sb'], wts['w_df'], gdq, gdk, *tab_c]
        pre_specs = [tok(d)] + [_full(a.shape) for a in pre_in[1:9]] + [tab()] * 4 \
            + [_full(a.shape) for a in pre_in[13:17]] + [tab()] * 3
        widths = (MLA_HEADS * LANES, MLA_HEADS * LANES, MLA_OUT, SB_OUT, SB_OUT, SB_OUT,
                  DIFF_QK, DIFF_QK, DIFF_OUT)
        qa, ka, va, qb, kb, vb, qc, kc, vc = pl.pallas_call(
            _pre_kernel, grid=(b, n_tm), in_specs=pre_specs,
            out_specs=[tok(wd) for wd in widths], out_shape=[act(wd) for wd in widths],
            compiler_params=cparams2, name="pre_tokens")(*pre_in)

        out_a = pl.pallas_call(
            functools.partial(_softmax2_kernel, mode='mla', p_len=p_len, tile=tqs),
            grid=(b, MLA_HEADS // 2),
            in_specs=[seq_block(2 * LANES), seq_block(2 * LANES), seq_block(LANES)],
            out_specs=seq_block(LANES), scratch_shapes=soft_scratch(2),
            out_shape=act(MLA_OUT), compiler_params=cparams2, name="attn_mla")(qa, ka, va)

        sb_per_step = t['sb_tiles_per_step']
        sb_rows = sb_per_step * tqb
        out_b = pl.pallas_call(
            functools.partial(_stick_kernel, tq=tqb, tk=tqb, tiles_per_step=sb_per_step),
            grid=(b, p_len // sb_rows),
            in_specs=[pl.BlockSpec((None, sb_rows, SB_OUT), lambda bi, qi: (bi, qi, 0)),
                      pl.BlockSpec((None, p_len, SB_OUT), lambda bi, qi: (bi, 0, 0)),
                      pl.BlockSpec((None, p_len, SB_OUT), lambda bi, qi: (bi, 0, 0)),
                      _full(tri2.shape)],
            out_specs=pl.BlockSpec((None, sb_rows, SB_OUT), lambda bi, qi: (bi, qi, 0)),
            scratch_shapes=[pltpu.VMEM((SB_HEADS * tqb, LANES), jnp.float32)] * 2,
            out_shape=act(SB_OUT), compiler_params=cparams2s, name="attn_stick")(qb, kb, vb, tri2)

        lam_init = 0.8 - 0.6 * math.exp(-0.3 * layer)
        g_c = _row(diff_out_norm_g[layer])
        out_c = pl.pallas_call(
            functools.partial(_softmax2_kernel, mode='diff', p_len=p_len, tile=tqs, lam_init=lam_init),
            grid=(b, DIFF_HEADS),
            in_specs=[seq_block(LANES), seq_block(LANES), seq_block(LANES),
                      _full((4, DIFF_HEAD_DIM)), _full((1, LANES))],
            out_specs=seq_block(LANES), scratch_shapes=soft_scratch(1),
            out_shape=act(DIFF_OUT), compiler_params=cparams2, name="attn_diff")(
                qc, kc, vc, diff_lambda[layer].astype(jnp.float32), g_c)

        g2 = _row(ln2_g[layer])
        post_in = [h_res, g1, wts['w_gate'], _row(gate_b[layer]), out_a, out_b, out_c, wts['wb'], wts['wo'],
                   g2, wts['w1'], wts['w2']]
        post_specs = [tok(d)] + [_full(a.shape) for a in post_in[1:4]] \
            + [tok(MLA_OUT), tok(SB_OUT), tok(DIFF_OUT)] + [_full(a.shape) for a in post_in[7:]]
        h_res = pl.pallas_call(
            functools.partial(_post_kernel, n_chunks=4), grid=(b, n_tm), in_specs=post_specs, out_specs=tok(d),
            out_shape=jax.ShapeDtypeStruct((b, p_len, d), jnp.float32),
            compiler_params=cparams2, name="merge_ffn")(*post_in)

    return h_res[:, PAD + N_META:]
```

```python
import functools
import math

import jax
import jax.numpy as jnp
from jax import lax
from jax.experimental import pallas as pl
from jax.experimental.pallas import tpu as pltpu

D_MODEL = 1024
DEPTH = 2
N_META = 16
BLOCK = 128
PAD = BLOCK - N_META
ROPE_THETA = 10000.0
NORM_EPS = 1e-6
MASK_VALUE = -1e30

MLA_HEADS = 8
MLA_Q_LORA = 256
MLA_KV_LORA = 128
MLA_NOPE = 64
MLA_ROPE = 32
MLA_V = 64
MLA_QK = MLA_NOPE + MLA_ROPE
SB_HEADS = 8
SB_HEAD_DIM = 64
DIFF_HEADS = 4
DIFF_HEAD_DIM = 64
DIFF_V_DIM = 2 * DIFF_HEAD_DIM
N_BRANCHES = 3
MLA_OUT = MLA_HEADS * MLA_V
SB_OUT = SB_HEADS * SB_HEAD_DIM
DIFF_QK = DIFF_HEADS * 2 * DIFF_HEAD_DIM
DIFF_OUT = DIFF_HEADS * DIFF_V_DIM
D_FF = 4 * D_MODEL
IN_SIZES = (MLA_Q_LORA, MLA_KV_LORA, MLA_ROPE, SB_OUT, SB_OUT, SB_OUT,
            DIFF_QK, DIFF_QK, DIFF_OUT, N_BRANCHES * D_MODEL)

LANES = 128
HALF_LANES = LANES // 2
MXU_DIM = 256
CHAIN = 128

LOG2E = math.log2(math.e)
VMEM_LIMIT_BYTES = 56 * 1024 * 1024
STICK_SKIP_LOG2 = 110.0 * LOG2E
STICK_HEAD_BLOCKS = 3

_NT = (((1,), (1,)), ((), ()))


def _tiles(p_len):
    assert p_len % 528 == 0 and (p_len - BLOCK) % (2 * MXU_DIM) == 0 and p_len % (11 * BLOCK) == 0, p_len
    return dict(tm=528, t_soft=2 * MXU_DIM, t_sb=BLOCK, sb_tiles_per_step=11, t_out=2 * MXU_DIM)


def _rms(x, g, n):
    ss = jnp.sum(x * x, axis=-1, keepdims=True) * (1.0 / n)
    return (x * lax.rsqrt(ss + NORM_EPS)) * g


def _rope_lanes(x, cos, sina, sinb, half):
    return x * cos + pltpu.roll(x, LANES - half, 1) * sina + pltpu.roll(x, half, 1) * sinb


def _bdot(a, b):
    return jnp.dot(a, b, preferred_element_type=jnp.float32)


def _dot_nt(a, b):
    return lax.dot_general(a, b, _NT, preferred_element_type=jnp.float32)


def _pre_kernel(x_ref, g1_ref, wlat_ref, gcq_ref, gckv_ref, wuq_ref, wuqp_ref, wk_ref, wv_ref,
                c1q_ref, c2q_ref, c1k_ref, c2k_ref,
                wsb_ref, wdf_ref, gdq_ref, gdk_ref, cosc_ref, sinac_ref, sinbc_ref,
                qa_ref, ka_ref, va_ref, qb_ref, kb_ref, vb_ref, qc_ref, kc_ref, vc_ref):
    bf = jnp.bfloat16
    x = x_ref[...]
    h = _rms(x, g1_ref[...], D_MODEL).astype(bf)

    df = _bdot(h, wdf_ref[...])
    cosc, sinac, sinbc = cosc_ref[...], sinac_ref[...], sinbc_ref[...]
    lane = lax.broadcasted_iota(jnp.int32, (1, LANES), 1)
    lo = lane < HALF_LANES
    c_scale = (DIFF_HEAD_DIM ** -0.5) * LOG2E

    def half_norm(t, g):
        sq = t * t
        ss_lo = jnp.sum(jnp.where(lo, sq, 0.0), axis=-1, keepdims=True)
        ss_hi = jnp.sum(jnp.where(lo, 0.0, sq), axis=-1, keepdims=True)
        r = jnp.where(lo, lax.rsqrt(ss_lo * (1.0 / DIFF_HEAD_DIM) + NORM_EPS),
                      lax.rsqrt(ss_hi * (1.0 / DIFF_HEAD_DIM) + NORM_EPS))
        return (t * r) * g

    for hh in range(DIFF_HEADS):
        sl = slice(hh * LANES, (hh + 1) * LANES)
        qh = _rope_lanes(half_norm(df[:, sl], gdq_ref[...]), cosc, sinac, sinbc, DIFF_HEAD_DIM // 2)
        qc_ref[:, sl] = (qh * c_scale).astype(bf)
        sk = slice(DIFF_QK + hh * LANES, DIFF_QK + (hh + 1) * LANES)
        kh = _rope_lanes(half_norm(df[:, sk], gdk_ref[...]), cosc, sinac, sinbc, DIFF_HEAD_DIM // 2)
        kc_ref[:, sl] = kh.astype(bf)
    vc_ref[...] = df[:, 2 * DIFF_QK:].astype(bf)

    lat = _bdot(h, wlat_ref[...])
    cq = _rms(lat[:, :MLA_Q_LORA], gcq_ref[...], MLA_Q_LORA).astype(bf)
    ckv = _rms(lat[:, MLA_Q_LORA:MLA_Q_LORA + MLA_KV_LORA], gckv_ref[...], MLA_KV_LORA).astype(bf)
    kr_off = MLA_Q_LORA + MLA_KV_LORA
    kr_block = lat[:, kr_off:kr_off + LANES]
    krp_c2 = lat[:, kr_off + LANES:] * c2k_ref[...]
    qf = _bdot(cq, wuq_ref[...])
    qpf = _bdot(cq, wuqp_ref[...])
    kf = _bdot(ckv, wk_ref[...])
    va_ref[...] = _bdot(ckv, wv_ref[...]).astype(bf)
    c1q, c2q, c1k = c1q_ref[...], c2q_ref[...], c1k_ref[...]

    def inv_rms(t):
        return lax.rsqrt(jnp.sum(t * t, axis=-1, keepdims=True) * (1.0 / MLA_QK) + NORM_EPS)

    for hh in range(MLA_HEADS):
        sl = slice(hh * LANES, (hh + 1) * LANES)
        qx = qf[:, sl]
        qa_ref[:, sl] = ((qx * c1q + qpf[:, sl] * c2q) * inv_rms(qx)).astype(bf)
        kx = kf[:, sl] + kr_block
        ka_ref[:, sl] = ((kx * c1k + krp_c2) * inv_rms(kx)).astype(bf)

    sb = _bdot(h, wsb_ref[...])
    qb_ref[...] = (sb[:, :SB_OUT] * ((SB_HEAD_DIM ** -0.5) * LOG2E)).astype(bf)
    kb_ref[...] = sb[:, SB_OUT:2 * SB_OUT].astype(bf)
    vb_ref[...] = sb[:, 2 * SB_OUT:].astype(bf)


def _softmax2_kernel(*refs, mode, p_len, tile, lam_init=None):
    if mode == 'mla':
        q_ref, k_ref, v_ref, o_ref, m_scr, acc_scr, s0_scr, s1_scr, mx0_scr, mx1_scr, vx_scr = refs
    else:
        q_ref, k_ref, v_ref, lam_ref, g_ref, o_ref, m_scr, acc_scr, s0_scr, s1_scr, mx0_scr, mx1_scr, vx_scr = refs
    bf = jnp.bfloat16
    kb = MXU_DIM
    assert tile % (2 * kb) == 0, (tile, kb)
    kpt = tile // kb
    n_tiles = (p_len - BLOCK) // tile
    n_kb = (p_len - BLOCK) // kb
    n_half = tile // CHAIN
    n_chain = 2 * n_half
    lane = lax.broadcasted_iota(jnp.int32, (1, LANES), 1)
    lo = lane < HALF_LANES

    if mode == 'diff':
        lp = lam_ref[...]
        lam = (jnp.exp(jnp.sum(lp[0:1] * lp[1:2], axis=-1, keepdims=True))
               - jnp.exp(jnp.sum(lp[2:3] * lp[3:4], axis=-1, keepdims=True)) + lam_init)

    def load_q(r0, n):
        rows = pl.ds(r0, n)
        if mode == 'mla':
            return q_ref[rows, :LANES], q_ref[rows, LANES:]
        q = q_ref[rows, :]
        return q, q

    def k_block(r0, n, m):
        rows = pl.ds(r0, n)
        if mode == 'mla':
            return k_ref[rows, m * LANES:(m + 1) * LANES]
        k = k_ref[rows, :]
        return jnp.where(lo if m == 0 else jnp.logical_not(lo), k, jnp.zeros_like(k))

    def combine(o0, o1):
        if mode == 'mla':
            return jnp.where(lo, o0, o1)
        return _rms(o0 - lam * o1, g_ref[...], DIFF_V_DIM) * (1.0 - lam_init)

    row = lax.broadcasted_iota(jnp.int32, (CHAIN, 1), 0)
    col0 = lax.broadcasted_iota(jnp.int32, (1, BLOCK), 1)
    meta_ok = col0 >= PAD

    q_meta = load_q(0, BLOCK)
    v_meta = v_ref[pl.ds(0, BLOCK), :]
    o_meta = []
    for m in range(2):
        s = jnp.where((col0 <= row) & meta_ok, _dot_nt(q_meta[m], k_block(0, BLOCK, m)), MASK_VALUE)
        p = jnp.exp2(s - jnp.max(s, axis=-1, keepdims=True))
        o_meta.append(_bdot(p.astype(bf), v_meta) / jnp.sum(p, axis=-1, keepdims=True))
    o_ref[pl.ds(0, BLOCK), :] = combine(*o_meta).astype(bf)

    col = lax.broadcasted_iota(jnp.int32, (1, kb), 1)
    diag = [[col + d * kb <= row + (c % n_half) * CHAIN for c in range(n_chain)] for d in range(kpt)]

    def key_row(j):
        return pl.multiple_of(BLOCK + j * kb, BLOCK)

    def qk_into(bufs, qs, r0, masks, h_from=0):
        s_scr, mx_scr = bufs
        for m in range(2):
            s = _dot_nt(qs[m][h_from * CHAIN:], k_block(r0, kb, m))
            for h in range(h_from, n_half):
                c = m * n_half + h
                rs = slice(c * CHAIN, (c + 1) * CHAIN)
                s_c = s[(h - h_from) * CHAIN:(h - h_from + 1) * CHAIN]
                if masks is not None:
                    s_c = jnp.where(masks[c], s_c, MASK_VALUE)
                s_scr[rs, :] = s_c
                mx_scr[rs, :] = jnp.broadcast_to(jnp.max(s_c, axis=-1, keepdims=True), (CHAIN, LANES))

    v_maps = 2 if mode == 'mla' else 1

    def v_ext(vblk, m):
        if mode == 'diff':
            return jnp.concatenate([vblk, jnp.ones_like(vblk)], axis=1)
        mine = lo if m == 0 else jnp.logical_not(lo)
        ones = jnp.broadcast_to(jnp.where(mine, 1.0, 0.0).astype(bf), vblk.shape)
        return jnp.concatenate([jnp.where(mine, vblk, jnp.zeros_like(vblk)), ones], axis=1)

    vx_meta_base = n_kb * v_maps * kb
    for m in range(v_maps):
        vx_scr[pl.ds(vx_meta_base + m * BLOCK, BLOCK), :] = v_ext(v_meta, m)

    def build_vx(j, c_):
        vblk = v_ref[pl.ds(key_row(j), kb), :]
        base = pl.multiple_of(j * v_maps * kb, kb)
        for m in range(v_maps):
            vx_scr[pl.ds(base + m * kb, kb), :] = v_ext(vblk, m)
        return c_

    lax.fori_loop(0, n_kb, build_vx, 0)

    def update(get_s, get_max, vx_base, n, h_from=0):
        rhs = vx_scr[pl.ds(vx_base, v_maps * n), :]
        for h in range(h_from, n_half):
            parts, alphas = [], []
            for m in range(2):
                c = m * n_half + h
                rs = slice(c * CHAIN, (c + 1) * CHAIN)
                s = get_s(c)
                m_run = m_scr[rs, :]
                m_new = jnp.maximum(m_run, get_max(c))
                alpha = jnp.exp2(m_run - m_new)
                m_scr[rs, :] = m_new
                ps = [jnp.exp2(s[:, k * LANES:(k + 1) * LANES] - m_new).astype(bf) for k in range(n // LANES)]
                if mode == 'diff':
                    p = ps[0] if len(ps) == 1 else jnp.concatenate(ps, axis=1)
                    acc_scr[rs, :] = jnp.concatenate([alpha, alpha], axis=1) * acc_scr[rs, :] + _bdot(p, rhs)
                parts += ps
                alphas.append(alpha)
            if mode == 'mla':
                both = jnp.where(lo, alphas[0], alphas[1])
                ah = slice(h * CHAIN, (h + 1) * CHAIN)
                acc_scr[ah, :] = (jnp.concatenate([both, both], axis=1) * acc_scr[ah, :]
                                  + _bdot(jnp.concatenate(parts, axis=1), rhs))

    def update_from(bufs, vx_base, h_from=0):
        s_scr, mx_scr = bufs
        update(lambda c: s_scr[c * CHAIN:(c + 1) * CHAIN, :], lambda c: mx_scr[c * CHAIN:(c + 1) * CHAIN, :],
               vx_base, kb, h_from)

    def vx_row(j):
        return pl.multiple_of(j * v_maps * kb, kb)

    buf0, buf1 = (s0_scr, mx0_scr), (s1_scr, mx1_scr)

    def q_tile(i, carry):
        r0 = pl.multiple_of(BLOCK + i * tile, BLOCK)
        qs = load_q(r0, tile)
        m_scr[...] = jnp.full(m_scr.shape, MASK_VALUE, jnp.float32)
        acc_scr[...] = jnp.zeros(acc_scr.shape, jnp.float32)
        first = i == 0
        qk_into(buf0, qs, key_row(0), [col <= row + (c % n_half) * CHAIN + i * tile for c in range(n_chain)])

        s_meta = [jnp.where(meta_ok, _dot_nt(qs[m], k_block(0, BLOCK, m)), MASK_VALUE) for m in range(2)]

        def meta_s(c):
            return s_meta[c // n_half][(c % n_half) * CHAIN:(c % n_half + 1) * CHAIN]

        update(meta_s, lambda c: jnp.max(meta_s(c), axis=-1, keepdims=True), vx_meta_base, BLOCK)

        def pair_at(j):
            qk_into(buf1, qs, key_row(j + 1), None)
            update_from(buf0, vx_row(j))
            qk_into(buf0, qs, key_row(j + 2), None)
            update_from(buf1, vx_row(j + 1))

        def quad(t, c_):
            pair_at(4 * t)
            pair_at(4 * t + 2)
            return c_

        jd = kpt * i
        n_pairs = jnp.maximum(lax.shift_right_logical(jd, 1) - 1, 0)
        lax.fori_loop(0, lax.shift_right_logical(n_pairs, 1), quad, 0)

        @pl.when((n_pairs & 1) == 1)
        def _():
            pair_at(2 * (n_pairs - 1))

        @pl.when(jnp.logical_not(first))
        def _():
            qk_into(buf1, qs, key_row(jd - 1), None)
            update_from(buf0, vx_row(jd - 2))
            qk_into(buf0, qs, key_row(jd), diag[0])
            update_from(buf1, vx_row(jd - 1))

        bufs = (buf0, buf1)
        for dg in range(kpt):
            if dg + 1 < kpt:
                qk_into(bufs[(dg + 1) % 2], qs, key_row(jd + dg + 1), diag[dg + 1], (dg + 1) * kb // CHAIN)
            update_from(bufs[dg % 2], vx_row(jd + dg), dg * kb // CHAIN)

        for h in range(n_half):
            a = acc_scr[h * CHAIN:(h + 1) * CHAIN, :]
            if mode == 'mla':
                res = a[:, :LANES] / a[:, LANES:]
            else:
                a1 = acc_scr[(n_half + h) * CHAIN:(n_half + h + 1) * CHAIN, :]
                res = combine(a[:, :LANES] / a[:, LANES:], a1[:, :LANES] / a1[:, LANES:])
            o_ref[pl.ds(r0 + h * CHAIN, CHAIN), :] = res.astype(bf)
        return carry

    lax.fori_loop(0, n_tiles, q_tile, 0)


def _stick_kernel(q_ref, k_ref, v_ref, tri_ref, o_ref, carry_scr, acc_scr, *, tq, tk, tiles_per_step):
    bf = jnp.bfloat16
    n_blocks = q_ref.shape[-1] // LANES
    lane = lax.broadcasted_iota(jnp.int32, (1, LANES), 1)
    lo = lane < HALF_LANES
    tri2 = tri_ref[...]

    def sub_tile(sub, c_):
        i = pl.program_id(1) * tiles_per_step + sub
        q_rows = pl.ds(pl.multiple_of(sub * tq, tq), tq)
        qs = []
        for blk in range(n_blocks):
            q = q_ref[q_rows, blk * LANES:(blk + 1) * LANES]
            zero = jnp.zeros_like(q)
            qs.append(jnp.concatenate([jnp.where(lo, q, zero), jnp.where(lo, zero, q)], axis=0))
        qpos = i * tq + (lax.broadcasted_iota(jnp.int32, (2 * tq, 1), 0) & (tq - 1))
        carry_scr[...] = jnp.zeros(carry_scr.shape, jnp.float32)
        acc_scr[...] = jnp.zeros(acc_scr.shape, jnp.float32)

        def blocks_step(js, flags):
            rows = [pl.ds(pl.multiple_of(j * tk, tk), tk) for j in js]
            masks = []
            for j, masked in zip(js, flags):
                kpos = j * tk + lax.broadcasted_iota(jnp.int32, (1, tk), 1)
                masks.append(((kpos < qpos) & (kpos >= PAD)) if masked else None)
            cols = [slice(blk * LANES, (blk + 1) * LANES) for blk in range(n_blocks)]
            zs = [[_dot_nt(qs[blk], k_ref[r, cols[blk]]) for blk in range(n_blocks)] for r in rows]
            cums, ksums = [], []
            for b in range(len(js)):
                cums.append([])
                ksums.append([])
                for blk in range(n_blocks):
                    z = zs[b][blk]
                    keep = -(jnp.maximum(z, 0.0) + jnp.log2(1.0 + jnp.exp2(-jnp.abs(z))))
                    if masks[b] is not None:
                        keep = jnp.where(masks[b], keep, 0.0)
                    hi = keep.astype(bf)
                    lw = (keep - hi.astype(jnp.float32)).astype(bf)
                    cums[b].append(_bdot(jnp.concatenate([hi, lw], axis=1), tri2))
                    ksums[b].append(jnp.sum(keep, axis=-1, keepdims=True))
            worst = None
            for blk in range(n_blocks):
                rs = slice(2 * blk * tq, 2 * (blk + 1) * tq)
                carry = carry_scr[rs, :]
                pv = None
                for b in range(len(js)):
                    a = jnp.exp2(zs[b][blk] + (carry + cums[b][blk]))
                    if masks[b] is not None:
                        a = jnp.where(masks[b], a, 0.0)
                    d = _bdot(a.astype(bf), v_ref[rows[b], cols[blk]])
                    pv = d if pv is None else pv + d
                    carry = carry + ksums[b][blk]
                carry_scr[rs, :] = carry
                acc_scr[rs, :] += pv
                worst = carry if worst is None else jnp.maximum(worst, carry)
            return jnp.max(worst)

        def cond(st):
            j, worst = st
            return (j >= 0) & (worst > -STICK_SKIP_LOG2)

        def body(st):
            j, _ = st
            worst = lax.cond((j == i) | (j == 0), lambda: blocks_step([j], [True]), lambda: blocks_step([j], [False]))
            return j - 1, worst

        head = STICK_HEAD_BLOCKS
        start = lax.cond(i >= head,
                         lambda: (i - head, blocks_step([i - b for b in range(head)], [True] + [False] * (head - 1))),
                         lambda: (i, jnp.float32(0.0)))
        lax.while_loop(cond, body, start)
        for blk in range(n_blocks):
            r0 = 2 * blk * tq
            o_ref[q_rows, blk * LANES:(blk + 1) * LANES] = jnp.where(
                lo, acc_scr[r0:r0 + tq, :], acc_scr[r0 + tq:r0 + 2 * tq, :]).astype(bf)
        return c_

    lax.fori_loop(0, tiles_per_step, sub_tile, 0)


def _post_kernel(x_ref, g1_ref, wg_ref, gb_ref, oa_ref, ob_ref, oc_ref, wb_ref, wo_ref,
                 g2_ref, w1_ref, w2_ref, y_ref, *, n_chunks):
    bf = jnp.bfloat16

    def tile(ref):
        return ref[...] if len(ref.shape) == 2 else ref[0]

    x = tile(x_ref)
    h = _rms(x, g1_ref[...], D_MODEL).astype(bf)
    outs = (oa_ref, ob_ref, oc_ref)
    merged = None
    for br in range(N_BRANCHES):
        cs = slice(br * D_MODEL, (br + 1) * D_MODEL)
        logits = _bdot(h, wg_ref[:, cs]) + gb_ref[:, cs]
        gate = 1.0 / (1.0 + jnp.exp(-logits))
        rs = slice(br * MLA_OUT, (br + 1) * MLA_OUT)
        term = gate * _bdot(tile(outs[br]), wb_ref[rs, :])
        merged = term if merged is None else merged + term
    x = x + _bdot(merged.astype(bf), wo_ref[...])

    h2 = _rms(x, g2_ref[...], D_MODEL).astype(bf)
    ck = D_FF // n_chunks
    acc = x
    for c in range(n_chunks):
        cs = slice(c * ck, (c + 1) * ck)
        f = jnp.maximum(_bdot(h2, w1_ref[:, cs]), 0.0)
        acc = acc + _bdot((f * f).astype(bf), w2_ref[cs, :])
    y_ref[...] = acc


def _rope_tables(p_len, d, blocks):
    half = d // 2
    pos = jnp.maximum(jnp.arange(p_len) - PAD, 0)
    inv_freq = jnp.exp(-math.log(ROPE_THETA) * (2.0 * jnp.arange(half, dtype=jnp.float32) / d))
    ang = pos.astype(jnp.float32)[:, None] * inv_freq[None, :]
    cos, sin = jnp.cos(ang), jnp.sin(ang)
    zero = jnp.zeros_like(sin)
    cos_p, sina_p, sinb_p, at = [], [], [], 0
    for off in blocks:
        gap = off - at
        cos_p += [jnp.ones((p_len, gap), jnp.float32), cos, cos]
        sina_p += [jnp.zeros((p_len, gap), jnp.float32), -sin, zero]
        sinb_p += [jnp.zeros((p_len, gap), jnp.float32), zero, sin]
        at = off + d
    cos_p.append(jnp.ones((p_len, LANES - at), jnp.float32))
    sina_p.append(jnp.zeros((p_len, LANES - at), jnp.float32))
    sinb_p.append(jnp.zeros((p_len, LANES - at), jnp.float32))
    return tuple(jnp.concatenate(parts, axis=1) for parts in (cos_p, sina_p, sinb_p))


def _swap_halves(t):
    half = t.shape[-1] // 2
    return jnp.concatenate([t[..., half:], t[..., :half]], axis=-1)


def _row(v):
    return v.reshape(1, -1).astype(jnp.float32)


def _full(shape):
    return pl.BlockSpec(shape, lambda *_: (0,) * len(shape), pipeline_mode=pl.Buffered(1))


def _layer_weights(layer, w_in, mla_w_uq, mla_w_ukv, w_branch, w_out, w_ff1, w_ff2):
    bf = jnp.bfloat16
    w = w_in[layer]
    offs = [0]
    for s in IN_SIZES:
        offs.append(offs[-1] + s)
    d = w.shape[0]
    w_kr = w[:, offs[2]:offs[3]]
    w_lat = jnp.concatenate([
        w[:, offs[0]:offs[2]],
        jnp.zeros((d, MLA_NOPE), w.dtype), w_kr,
        jnp.zeros((d, LANES - MLA_QK), w.dtype),
        jnp.zeros((d, MLA_NOPE), w.dtype), _swap_halves(w_kr),
        jnp.zeros((d, LANES - MLA_QK), w.dtype)], axis=1).astype(bf)
    w_sb = w[:, offs[3]:offs[6]].astype(bf)
    w_df = w[:, offs[6]:offs[9]].astype(bf)
    w_gate = w[:, offs[9]:offs[10]].astype(bf)
    wuq = mla_w_uq[layer].reshape(MLA_Q_LORA, MLA_HEADS, MLA_QK)
    wuqp = jnp.concatenate([wuq[..., :MLA_NOPE], _swap_halves(wuq[..., MLA_NOPE:])], axis=-1)
    wuq = jnp.pad(wuq, ((0, 0), (0, 0), (0, LANES - MLA_QK))).reshape(MLA_Q_LORA, MLA_HEADS * LANES).astype(bf)
    wuqp = jnp.pad(wuqp, ((0, 0), (0, 0), (0, LANES - MLA_QK))).reshape(MLA_Q_LORA, MLA_HEADS * LANES).astype(bf)
    wukv = mla_w_ukv[layer].reshape(MLA_KV_LORA, MLA_HEADS, MLA_NOPE + MLA_V)
    wk = jnp.pad(wukv[:, :, :MLA_NOPE], ((0, 0), (0, 0), (0, LANES - MLA_NOPE)))
    wk = wk.reshape(MLA_KV_LORA, MLA_HEADS * LANES).astype(bf)
    wv = wukv[:, :, MLA_NOPE:].reshape(MLA_KV_LORA, MLA_OUT).astype(bf)
    return dict(w_lat=w_lat, w_sb=w_sb, w_df=w_df, w_gate=w_gate, wuq=wuq, wuqp=wuqp, wk=wk, wv=wv,
                wb=w_branch[layer].astype(bf), wo=w_out[layer].astype(bf),
                w1=w_ff1[layer].astype(bf), w2=w_ff2[layer].astype(bf))


def kernel(x, meta_tokens, ln1_g, w_in, mla_cq_norm_g, mla_ckv_norm_g, mla_w_uq, mla_w_ukv,
           mla_q_norm_g, mla_k_norm_g, diff_q_norm_g, diff_k_norm_g, diff_lambda,
           diff_out_norm_g, gate_b, w_branch, w_out, ln2_g, w_ff1, w_ff2):
    b, seq, d = x.shape
    assert d == D_MODEL
    bf = jnp.bfloat16
    pad = jnp.zeros((b, PAD, d), x.dtype)
    meta = jnp.broadcast_to(meta_tokens.astype(x.dtype)[None], (b, N_META, d))
    h_res = jnp.concatenate([pad, meta, x], axis=1)
    p_len = h_res.shape[1]
    t = _tiles(p_len)
    tm, tqs, tqb = t['tm'], t['t_soft'], t['t_sb']
    n_tm = p_len // tm

    tab_a = _rope_tables(p_len, MLA_ROPE, [MLA_NOPE])
    tab_c = _rope_tables(p_len, DIFF_HEAD_DIM, [0, HALF_LANES])
    tri = (jnp.arange(tqb)[:, None] >= jnp.arange(tqb)[None, :]).astype(bf)
    tri2 = jnp.concatenate([tri, tri], axis=0)

    cparams2 = pltpu.CompilerParams(dimension_semantics=("parallel", "parallel"),
                                    vmem_limit_bytes=VMEM_LIMIT_BYTES)
    cparams2s = pltpu.CompilerParams(dimension_semantics=("parallel", "arbitrary"),
                                     vmem_limit_bytes=VMEM_LIMIT_BYTES)

    def tok(width):
        return pl.BlockSpec((None, tm, width), lambda bi, ti: (bi, ti, 0))

    t_out = t['t_out']
    assert seq % t_out == 0, (seq, t_out)

    def real_tok(width):
        return pl.BlockSpec((pl.Element(1), pl.Element(t_out), pl.Element(width)),
                            lambda bi, ti: (bi, pl.multiple_of(PAD + N_META + ti * t_out, BLOCK), 0))

    def tab():
        return pl.BlockSpec((tm, LANES), lambda bi, ti: (ti, 0))

    def act(width):
        return jax.ShapeDtypeStruct((b, p_len, width), bf)

    def soft_scratch(v_maps):
        return [pltpu.VMEM((2 * tqs, LANES), jnp.float32),
                pltpu.VMEM((2 * tqs, 2 * LANES), jnp.float32),
                pltpu.VMEM((2 * tqs, MXU_DIM), jnp.float32), pltpu.VMEM((2 * tqs, MXU_DIM), jnp.float32),
                pltpu.VMEM((2 * tqs, LANES), jnp.float32), pltpu.VMEM((2 * tqs, LANES), jnp.float32),
                pltpu.VMEM((v_maps * p_len, 2 * LANES), bf)]

    def seq_block(width):
        return pl.BlockSpec((None, p_len, width), lambda bi, ci: (bi, 0, ci))

    for layer in range(DEPTH):
        wts = _layer_weights(layer, w_in, mla_w_uq, mla_w_ukv, w_branch, w_out, w_ff1, w_ff2)
        g1 = _row(ln1_g[layer])
        gdq = _row(jnp.tile(diff_q_norm_g[layer], 2))
        gdk = _row(jnp.tile(diff_k_norm_g[layer], 2))

        def mla_tables(gain, scale):
            g = gain.astype(jnp.float32)
            g_own = _row(jnp.pad(g, (0, LANES - MLA_QK)))
            g_partner = _row(jnp.pad(jnp.concatenate([g[:MLA_NOPE], _swap_halves(g[MLA_NOPE:])]),
                                     (0, LANES - MLA_QK)))
            return tab_a[0] * (g_own * scale), (tab_a[1] + tab_a[2]) * (g_partner * scale)

        c1q, c2q = mla_tables(mla_q_norm_g[layer], (MLA_QK ** -0.5) * LOG2E)
        c1k, c2k = mla_tables(mla_k_norm_g[layer], 1.0)

        pre_in = [h_res, g1, wts['w_lat'], _row(mla_cq_norm_g[layer]), _row(mla_ckv_norm_g[layer]),
                  wts['wuq'], wts['wuqp'], wts['wk'], wts['wv'], c1q, c2q, c1k, c2k,
                  wts['w_sb'], wts['w_df'], gdq, gdk, *tab_c]
        pre_specs = [tok(d)] + [_full(a.shape) for a in pre_in[1:9]] + [tab()] * 4 \
            + [_full(a.shape) for a in pre_in[13:17]] + [tab()] * 3
        widths = (MLA_HEADS * LANES, MLA_HEADS * LANES, MLA_OUT, SB_OUT, SB_OUT, SB_OUT,
                  DIFF_QK, DIFF_QK, DIFF_OUT)
        qa, ka, va, qb, kb, vb, qc, kc, vc = pl.pallas_call(
            _pre_kernel, grid=(b, n_tm), in_specs=pre_specs,
            out_specs=[tok(wd) for wd in widths], out_shape=[act(wd) for wd in widths],
            compiler_params=cparams2, name="pre_tokens")(*pre_in)

        out_a = pl.pallas_call(
            functools.partial(_softmax2_kernel, mode='mla', p_len=p_len, tile=tqs),
            grid=(b, MLA_HEADS // 2),
            in_specs=[seq_block(2 * LANES), seq_block(2 * LANES), seq_block(LANES)],
            out_specs=seq_block(LANES), scratch_shapes=soft_scratch(2),
            out_shape=act(MLA_OUT), compiler_params=cparams2, name="attn_mla")(qa, ka, va)

        sb_per_step = t['sb_tiles_per_step']
        sb_rows = sb_per_step * tqb
        out_b = pl.pallas_call(
            functools.partial(_stick_kernel, tq=tqb, tk=tqb, tiles_per_step=sb_per_step),
            grid=(b, p_len // sb_rows),
            in_specs=[pl.BlockSpec((None, sb_rows, SB_OUT), lambda bi, qi: (bi, qi, 0)),
                      pl.BlockSpec((None, p_len, SB_OUT), lambda bi, qi: (bi, 0, 0)),
                      pl.BlockSpec((None, p_len, SB_OUT), lambda bi, qi: (bi, 0, 0)),
                      _full(tri2.shape)],
            out_specs=pl.BlockSpec((None, sb_rows, SB_OUT), lambda bi, qi: (bi, qi, 0)),
            scratch_shapes=[pltpu.VMEM((SB_HEADS * tqb, LANES), jnp.float32)] * 2,
            out_shape=act(SB_OUT), compiler_params=cparams2s, name="attn_stick")(qb, kb, vb, tri2)

        lam_init = 0.8 - 0.6 * math.exp(-0.3 * layer)
        g_c = _row(diff_out_norm_g[layer])
        out_c = pl.pallas_call(
            functools.partial(_softmax2_kernel, mode='diff', p_len=p_len, tile=tqs, lam_init=lam_init),
            grid=(b, DIFF_HEADS),
            in_specs=[seq_block(LANES), seq_block(LANES), seq_block(LANES),
                      _full((4, DIFF_HEAD_DIM)), _full((1, LANES))],
            out_specs=seq_block(LANES), scratch_shapes=soft_scratch(1),
            out_shape=act(DIFF_OUT), compiler_params=cparams2, name="attn_diff")(
                qc, kc, vc, diff_lambda[layer].astype(jnp.float32), g_c)

        g2 = _row(ln2_g[layer])
        post_in = [h_res, g1, wts['w_gate'], _row(gate_b[layer]), out_a, out_b, out_c, wts['wb'], wts['wo'],
                   g2, wts['w1'], wts['w2']]
        last = layer == DEPTH - 1
        rows_in = real_tok if last else tok
        post_specs = [rows_in(d)] + [_full(a.shape) for a in post_in[1:4]] \
            + [rows_in(MLA_OUT), rows_in(SB_OUT), rows_in(DIFF_OUT)] + [_full(a.shape) for a in post_in[7:]]
        h_res = pl.pallas_call(
            functools.partial(_post_kernel, n_chunks=4), grid=(b, seq // t_out if last else n_tm),
            in_specs=post_specs,
            out_specs=pl.BlockSpec((None, t_out, d), lambda bi, ti: (bi, ti, 0)) if last else tok(d),
            out_shape=jax.ShapeDtypeStruct((b, seq if last else p_len, d), jnp.float32),
            compiler_params=cparams2, name="merge_ffn")(*post_in)

    return h_res
```
